```python
import jax, jax.numpy as jnp
from jax import lax
import numpy as np

D_MODEL = 4096
BATCH = 4
SEQ = 2048
DEPTH = 2
DEC_BATCH = 128
DEC_SEQ = 8
PAST_LEN = 16384
PAGE_SIZE = 128

N_MIXERS = 2
N_CONV_LAYERS = (DEPTH + 1) // 2
N_CHUNK_LAYERS = DEPTH // 2
CONV_WIDTH = 3
CONV_STATE = CONV_WIDTH - 1
CHUNK = 128
SG_WIDTH = D_MODEL
SG_GROUPS = 8
SG_HEAD_DIM = SG_WIDTH // SG_GROUPS
N_EXPERTS = 32
TOP_K = 4
D_EXPERT = D_MODEL
SWIGLU_LIMIT = 7.0
SWIGLU_ALPHA = 1.702
N_MOD = 6
EPS = 1e-6

kernel_name = "hybrid_conv_chunkmlp_moe_adaln_step"


def rmsnorm(x, g):
    xf = x.astype(jnp.float32)
    y = xf * lax.rsqrt(jnp.mean(xf * xf, axis=-1, keepdims=True) + EPS)
    return (y * g.astype(jnp.float32)).astype(x.dtype)


def layernorm(x, g, b):
    xf = x.astype(jnp.float32)
    xc = xf - jnp.mean(xf, axis=-1, keepdims=True)
    var = jnp.mean(xc * xc, axis=-1, keepdims=True)
    y = xc * lax.rsqrt(var + EPS) * g.astype(jnp.float32) + b.astype(jnp.float32)
    return y.astype(x.dtype)


def adaln(c, w, b):
    m = jax.nn.silu(c) @ w + b
    return jnp.split(m[:, None, :], N_MOD, axis=-1)


def modulate(h, shift, scale):
    return h * (1.0 + scale) + shift


def conv_mixer(h, conv_state, w_in, w_conv, w_out):
    T = h.shape[1]
    b_gate, c_gate, z = jnp.split(h @ w_in, 3, axis=-1)
    cz = c_gate * z
    z_ext = jnp.concatenate([conv_state.astype(cz.dtype), cz], axis=1)
    conv = w_conv[0] * z_ext[:, 0:T]
    for k in range(1, CONV_WIDTH):
        conv = conv + w_conv[k] * z_ext[:, k:k + T]
    return (b_gate * conv) @ w_out, z_ext[:, -CONV_STATE:]


def chunk_mixer(h, w_proj, b_proj, ln_g, ln_b, w_s, b_s, w_out):
    B, T, _ = h.shape
    L = min(T, CHUNK)
    n_chunks = T // L
    z = jax.nn.gelu(h @ w_proj + b_proj, approximate=False)
    u, v = jnp.split(z, 2, axis=-1)
    v = layernorm(v, ln_g, ln_b)
    causal = jnp.tril(jnp.ones((L, L), dtype=bool))
    ws = jnp.where(causal, w_s[:, :L, :L], jnp.zeros((), w_s.dtype))
    vc = v.reshape(B, n_chunks, L, SG_GROUPS, SG_HEAD_DIM)
    mixed = jnp.einsum("gts,bcsgd->bctgd", ws, vc) + b_s[:, :L].T[:, :, None]
    out = (u * mixed.reshape(B, T, SG_WIDTH)) @ w_out
    return out, v


def moe(h, w_router, b_router, w_gate, b_gate, w_up, b_up, w_down, b_down):
    logits = h.astype(jnp.float32) @ w_router.astype(jnp.float32) + b_router.astype(jnp.float32)
    top_vals, top_idx = lax.top_k(logits, TOP_K)
    probs = jax.nn.softmax(top_vals, axis=-1)
    combine = jnp.sum(jax.nn.one_hot(top_idx, N_EXPERTS, dtype=jnp.float32) * probs[..., None], axis=1)

    def expert_step(acc, xs):
        wg, bg, wu, bu, wd, bd, cw = xs
        g = jnp.minimum(h @ wg + bg, SWIGLU_LIMIT)
        u = jnp.clip(h @ wu + bu, -SWIGLU_LIMIT, SWIGLU_LIMIT)
        a = (u + 1.0) * (g * jax.nn.sigmoid(SWIGLU_ALPHA * g))
        y = a @ wd + bd
        return acc + cw[:, None].astype(acc.dtype) * y, None

    acc, _ = lax.scan(expert_step, jnp.zeros_like(h),
                      (w_gate, b_gate, w_up, b_up, w_down, b_down, combine.T))
    return acc


def setup_inputs(seed: int = 0) -> dict:
    key = jax.random.key(seed)
    ks = iter(jax.random.split(key, 40))
    D = D_MODEL

    def nrm(shape, scale):
        return jax.random.normal(next(ks), shape, jnp.float32) * scale

    return {
        "x_prompt": nrm((BATCH, SEQ, D), 1.0),
        "x_sample": nrm((DEC_BATCH, DEC_SEQ, D), 1.0),
        "c_prompt": nrm((BATCH, D), 1.0),
        "c_sample": nrm((DEC_BATCH, D), 1.0),
        "state_conv": nrm((N_CONV_LAYERS, DEC_BATCH, CONV_STATE, D), 1.0),
        "norm_mix_g": 1.0 + nrm((DEPTH, D), 0.02),
        "norm_ffn_g": 1.0 + nrm((DEPTH, D), 0.02),
        "ada_w": nrm((DEPTH, D, N_MOD * D), 0.5 * D ** -0.5),
        "ada_b": nrm((DEPTH, N_MOD * D), 0.02),
        "conv_w_in": nrm((N_CONV_LAYERS, D, 3 * D), D ** -0.5),
        "conv_w": nrm((N_CONV_LAYERS, CONV_WIDTH, D), CONV_WIDTH ** -0.5),
        "conv_w_out": nrm((N_CONV_LAYERS, D, D), D ** -0.5),
        "sg_w_proj": nrm((N_CHUNK_LAYERS, D, 2 * SG_WIDTH), D ** -0.5),
        "sg_b_proj": nrm((N_CHUNK_LAYERS, 2 * SG_WIDTH), 0.02),
        "sg_ln_g": 1.0 + nrm((N_CHUNK_LAYERS, SG_WIDTH), 0.02),
        "sg_ln_b": nrm((N_CHUNK_LAYERS, SG_WIDTH), 0.02),
        "sg_w_s": nrm((N_CHUNK_LAYERS, SG_GROUPS, CHUNK, CHUNK), CHUNK ** -0.5),
        "sg_b_s": 1.0 + nrm((N_CHUNK_LAYERS, SG_GROUPS, CHUNK), 0.02),
        "sg_w_out": nrm((N_CHUNK_LAYERS, SG_WIDTH, D), SG_WIDTH ** -0.5),
        "router_w": nrm((DEPTH, D, N_EXPERTS), D ** -0.5),
        "router_b": nrm((DEPTH, N_EXPERTS), 0.01),
        "exp_w_gate": nrm((DEPTH, N_EXPERTS, D, D_EXPERT), D ** -0.5),
        "exp_b_gate": nrm((DEPTH, N_EXPERTS, D_EXPERT), 0.02),
        "exp_w_up": nrm((DEPTH, N_EXPERTS, D, D_EXPERT), D ** -0.5),
        "exp_b_up": nrm((DEPTH, N_EXPERTS, D_EXPERT), 0.02),
        "exp_w_down": nrm((DEPTH, N_EXPERTS, D_EXPERT, D), D_EXPERT ** -0.5),
        "exp_b_down": nrm((DEPTH, N_EXPERTS, D), 0.02),
        "final_norm_g": 1.0 + nrm((D,), 0.02),
    }


def reference(x_prompt, x_sample, c_prompt, c_sample, state_conv,
              norm_mix_g, norm_ffn_g, ada_w, ada_b,
              conv_w_in, conv_w, conv_w_out,
              sg_w_proj, sg_b_proj, sg_ln_g, sg_ln_b, sg_w_s, sg_b_s, sg_w_out,
              router_w, router_b, exp_w_gate, exp_b_gate, exp_w_up, exp_b_up,
              exp_w_down, exp_b_down, final_norm_g):
    xp, xs = x_prompt, x_sample
    Bp, Sp, D = xp.shape
    Bs, Ss, _ = xs.shape
    n_prompt_tok = Bp * Sp
    conv_new_p, conv_new_s, chunk_new_s = [], [], []
    for i in range(DEPTH):
        mp = adaln(c_prompt, ada_w[i], ada_b[i])
        ms = adaln(c_sample, ada_w[i], ada_b[i])
        hp = modulate(rmsnorm(xp, norm_mix_g[i]), mp[0], mp[1])
        hs = modulate(rmsnorm(xs, norm_mix_g[i]), ms[0], ms[1])
        j = i // N_MIXERS
        if i % N_MIXERS == 0:
            zero_state = jnp.zeros((Bp, CONV_STATE, D), hp.dtype)
            op, stp = conv_mixer(hp, zero_state, conv_w_in[j], conv_w[j], conv_w_out[j])
            os_, sts = conv_mixer(hs, state_conv[j], conv_w_in[j], conv_w[j], conv_w_out[j])
            conv_new_p.append(stp)
            conv_new_s.append(sts)
        else:
            op, _ = chunk_mixer(hp, sg_w_proj[j], sg_b_proj[j], sg_ln_g[j], sg_ln_b[j],
                                sg_w_s[j], sg_b_s[j], sg_w_out[j])
            os_, vs = chunk_mixer(hs, sg_w_proj[j], sg_b_proj[j], sg_ln_g[j], sg_ln_b[j],
                                  sg_w_s[j], sg_b_s[j], sg_w_out[j])
            chunk_new_s.append(vs)
        xp = xp + mp[2] * op
        xs = xs + ms[2] * os_
        hp = modulate(rmsnorm(xp, norm_ffn_g[i]), mp[3], mp[4])
        hs = modulate(rmsnorm(xs, norm_ffn_g[i]), ms[3], ms[4])
        tokens = jnp.concatenate([hp.reshape(-1, D), hs.reshape(-1, D)], axis=0)
        f = moe(tokens, router_w[i], router_b[i], exp_w_gate[i], exp_b_gate[i],
                exp_w_up[i], exp_b_up[i], exp_w_down[i], exp_b_down[i])
        xp = xp + mp[5] * f[:n_prompt_tok].reshape(Bp, Sp, D)
        xs = xs + ms[5] * f[n_prompt_tok:].reshape(Bs, Ss, D)
    y_prompt = rmsnorm(xp, final_norm_g)
    y_sample = rmsnorm(xs, final_norm_g)
    return (y_prompt, y_sample, jnp.stack(conv_new_p), jnp.stack(conv_new_s), jnp.stack(chunk_new_s))
```

```python
import functools

import jax
import jax.numpy as jnp
from jax import lax
from jax.experimental import pallas as pl
from jax.experimental.pallas import tpu as pltpu

F32 = jnp.float32
BF16 = jnp.bfloat16

EPS = 1e-6
TOP_K = 4
SWIGLU_LIMIT = 7.0
SWIGLU_ALPHA = 1.702
N_MOD = 6
LANES = 128
SLAB = 128
VMEM_LIMIT = 56 * 1024 * 1024

TM = 1024
TM_E = 256
TM_NM = 256
TM_C = 128


def _params(sem, vmem=VMEM_LIMIT):
    return pltpu.CompilerParams(dimension_semantics=sem, vmem_limit_bytes=vmem)


def _dot(a, b):
    return jnp.dot(a, b, preferred_element_type=F32)


def _slab_bcast(y, row_fn):
    tm, n = y.shape
    return row_fn(y.reshape(tm // SLAB, SLAB, n)).reshape(tm, n)


def _ada_kernel(c_ref, w_ref, b_ref, o_ref):
    c = c_ref[...]
    s = (c * jax.nn.sigmoid(c)).astype(BF16)
    o_ref[0] = _dot(s, w_ref[0]) + b_ref[0]


def _ada(c_all, ada_w, ada_b):
    depth, d, n6 = ada_w.shape
    rows = c_all.shape[0]
    tn = 1024
    return pl.pallas_call(
        _ada_kernel,
        grid=(depth, n6 // tn),
        in_specs=[
            pl.BlockSpec((rows, d), lambda l, n: (0, 0)),
            pl.BlockSpec((1, d, tn), lambda l, n: (l, 0, n)),
            pl.BlockSpec((1, 1, tn), lambda l, n: (l, 0, n)),
        ],
        out_specs=pl.BlockSpec((1, rows, tn), lambda l, n: (l, 0, n)),
        out_shape=jax.ShapeDtypeStruct((depth, rows, n6), F32),
        compiler_params=_params(("arbitrary", "arbitrary")),
        name="ada",
    )(c_all, ada_w, ada_b.reshape(depth, 1, n6))


def _rms(x, g):
    return x * lax.rsqrt(jnp.mean(x * x, axis=-1, keepdims=True) + EPS) * g


def _normmod_kernel(x_ref, g_ref, sh_ref, sc_ref, o_ref):
    y = _rms(x_ref[...], g_ref[...])
    h = _slab_bcast(y, lambda y3: y3 * (1.0 + sc_ref[...])[None] + sh_ref[...][None])
    o_ref[...] = h.astype(o_ref.dtype)


def _mod_rowblock(tm, n_prompt_rows, seq_len):
    n_prompt_tiles = n_prompt_rows // tm
    tiles_per_seq = seq_len // tm
    n_seq = n_prompt_rows // seq_len
    return lambda m: jnp.where(m < n_prompt_tiles, m // tiles_per_seq, n_seq)


def _normmod(x, g, mods, k_shift, k_scale, rb, out_dtype):
    t, d = x.shape
    tm = TM_NM
    return pl.pallas_call(
        _normmod_kernel,
        grid=(t // tm,),
        in_specs=[
            pl.BlockSpec((tm, d), lambda m: (m, 0)),
            pl.BlockSpec((1, d), lambda m: (0, 0)),
            pl.BlockSpec((SLAB, d), lambda m: (rb(m), k_shift)),
            pl.BlockSpec((SLAB, d), lambda m: (rb(m), k_scale)),
        ],
        out_specs=pl.BlockSpec((tm, d), lambda m: (m, 0)),
        out_shape=jax.ShapeDtypeStruct((t, d), out_dtype),
        compiler_params=_params(("arbitrary",)),
        name="normmod",
    )(x, g.reshape(1, d), mods, mods)


def _normmod_router_kernel(n_experts, x_ref, g_ref, sh_ref, sc_ref, wr_ref, br_ref, h_ref, idx_ref, p_ref):
    y = _rms(x_ref[...], g_ref[...])
    h = _slab_bcast(y, lambda y3: y3 * (1.0 + sc_ref[...])[None] + sh_ref[...][None])
    h_ref[...] = h
    logits = jnp.dot(h, wr_ref[...], preferred_element_type=F32, precision=lax.Precision.HIGHEST) + br_ref[...]
    lane = lax.broadcasted_iota(jnp.int32, logits.shape, 1)
    neg = jnp.float32(-jnp.inf)
    l = jnp.where(lane < n_experts, logits, neg)
    vals, idxs = [], []
    for _ in range(TOP_K):
        mx = jnp.max(l, axis=-1, keepdims=True)
        ix = jnp.min(jnp.where(l == mx, lane, LANES), axis=-1, keepdims=True)
        vals.append(mx)
        idxs.append(ix)
        l = jnp.where(lane == ix, neg, l)
    es = [jnp.exp(v - vals[0]) for v in vals]
    tot = es[0] + es[1] + es[2] + es[3]
    idx_out = jnp.zeros(logits.shape, jnp.int32)
    p_out = jnp.zeros(logits.shape, F32)
    for k in range(TOP_K):
        idx_out = jnp.where(lane == k, idxs[k], idx_out)
        p_out = jnp.where(lane == k, es[k] / tot, p_out)
    idx_ref[...] = idx_out
    p_ref[...] = p_out


def _normmod_router(x, g, mods, k_shift, k_scale, rb, w_router, b_router):
    t, d = x.shape
    n_experts = w_router.shape[1]
    tm = TM_NM
    wr = jnp.pad(w_router, ((0, 0), (0, LANES - n_experts)))
    br = jnp.pad(b_router, (0, LANES - n_experts)).reshape(1, LANES)
    return pl.pallas_call(
        functools.partial(_normmod_router_kernel, n_experts),
        grid=(t // tm,),
        in_specs=[
            pl.BlockSpec((tm, d), lambda m: (m, 0)),
            pl.BlockSpec((1, d), lambda m: (0, 0)),
            pl.BlockSpec((SLAB, d), lambda m: (rb(m), k_shift)),
            pl.BlockSpec((SLAB, d), lambda m: (rb(m), k_scale)),
            pl.BlockSpec((d, LANES), lambda m: (0, 0)),
            pl.BlockSpec((1, LANES), lambda m: (0, 0)),
        ],
        out_specs=[
            pl.BlockSpec((tm, d), lambda m: (m, 0)),
            pl.BlockSpec((tm, LANES), lambda m: (m, 0)),
            pl.BlockSpec((tm, LANES), lambda m: (m, 0)),
        ],
        out_shape=[
            jax.ShapeDtypeStruct((t, d), F32),
            jax.ShapeDtypeStruct((t, LANES), jnp.int32),
            jax.ShapeDtypeStruct((t, LANES), F32),
        ],
        compiler_params=_params(("arbitrary",)),
        name="normmod_router",
    )(x, g.reshape(1, d), mods, mods, wr, br)


def _final_norm_kernel(x_ref, g_ref, o_ref):
    o_ref[...] = _rms(x_ref[...], g_ref[...])


def _final_norm(x, g):
    t, d = x.shape
    tm = TM_NM
    return pl.pallas_call(
        _final_norm_kernel,
        grid=(t // tm,),
        in_specs=[pl.BlockSpec((tm, d), lambda m: (m, 0)), pl.BlockSpec((1, d), lambda m: (0, 0))],
        out_specs=pl.BlockSpec((tm, d), lambda m: (m, 0)),
        out_shape=jax.ShapeDtypeStruct((t, d), F32),
        compiler_params=_params(("arbitrary",)),
        name="final_norm",
    )(x, g.reshape(1, d))


def _conv_in_kernel(n_prompt_tiles, tiles_per_seq, h_ref, wb_ref, wc_ref, wz_ref, cw_ref, st_ref,
                    u_ref, cz_ref, carry_ref):
    m = pl.program_id(0)
    n = pl.program_id(1)
    h = h_ref[...]
    b = _dot(h, wb_ref[...])
    cz = _dot(h, wc_ref[...]) * _dot(h, wz_ref[...])
    cz_ref[...] = cz
    tm, tn = cz.shape
    w0 = cw_ref[0:1, :]
    w1 = cw_ref[1:2, :]
    w2 = cw_ref[2:3, :]

    @pl.when(m < n_prompt_tiles)
    def _():
        @pl.when(m % tiles_per_seq == 0)
        def _():
            carry_ref[n] = jnp.zeros((8, tn), F32)

        prev = carry_ref[n]
        row = lax.broadcasted_iota(jnp.int32, (tm, tn), 0)
        p1 = jnp.where(row == 0, prev[7:8, :], pltpu.roll(cz, 1, 0))
        p2 = jnp.where(row == 0, prev[6:7, :], jnp.where(row == 1, prev[7:8, :], pltpu.roll(cz, 2, 0)))
        u_ref[...] = (b * (w0 * p2 + w1 * p1 + w2 * cz)).astype(u_ref.dtype)
        carry_ref[n] = cz[tm - 8:, :]

    @pl.when(m >= n_prompt_tiles)
    def _():
        st0 = st_ref[0]
        st1 = st_ref[1]
        p1 = jnp.concatenate([st1, cz[:tm - SLAB, :]], axis=0)
        p2 = jnp.concatenate([st0, st1, cz[:tm - 2 * SLAB, :]], axis=0)
        u_ref[...] = (b * (w0 * p2 + w1 * p1 + w2 * cz)).astype(u_ref.dtype)


def _conv_in(h, w_in, conv_w, state, n_prompt_rows, seq_len):
    t, d = h.shape
    tm, tn = TM, 256
    nb = d // tn
    kern = functools.partial(_conv_in_kernel, n_prompt_rows // tm, seq_len // tm)
    return pl.pallas_call(
        kern,
        grid=(t // tm, nb),
        in_specs=[
            pl.BlockSpec((tm, d), lambda m, n: (m, 0)),
            pl.BlockSpec((d, tn), lambda m, n: (0, n)),
            pl.BlockSpec((d, tn), lambda m, n: (0, n + nb)),
            pl.BlockSpec((d, tn), lambda m, n: (0, n + 2 * nb)),
            pl.BlockSpec((3, tn), lambda m, n: (0, n)),
            pl.BlockSpec((2, SLAB, tn), lambda m, n: (0, 0, n)),
        ],
        out_specs=[
            pl.BlockSpec((tm, tn), lambda m, n: (m, n)),
            pl.BlockSpec((tm, tn), lambda m, n: (m, n)),
        ],
        out_shape=[jax.ShapeDtypeStruct((t, d), BF16), jax.ShapeDtypeStruct((t, d), F32)],
        scratch_shapes=[pltpu.VMEM((nb, 8, tn), F32)],
        compiler_params=_params(("arbitrary", "arbitrary")),
        name="conv_in",
    )(h, w_in, w_in, w_in, conv_w, state)


def _mm_res_kernel(a_ref, w_ref, x_ref, gate_ref, o_ref):
    acc = _dot(a_ref[...], w_ref[...])
    o_ref[...] = x_ref[...] + _slab_bcast(acc, lambda a3: a3 * gate_ref[...][None])


def _mm_res(a, w, x, mods, k_gate, rb):
    t, d = x.shape
    kdim = a.shape[1]
    tm, tn = TM, 512
    nb = d // tn
    return pl.pallas_call(
        _mm_res_kernel,
        grid=(t // tm, nb),
        in_specs=[
            pl.BlockSpec((tm, kdim), lambda m, n: (m, 0)),
            pl.BlockSpec((kdim, tn), lambda m, n: (0, n)),
            pl.BlockSpec((tm, tn), lambda m, n: (m, n)),
            pl.BlockSpec((SLAB, tn), lambda m, n: (rb(m), k_gate * nb + n)),
        ],
        out_specs=pl.BlockSpec((tm, tn), lambda m, n: (m, n)),
        out_shape=jax.ShapeDtypeStruct((t, d), F32),
        compiler_params=_params(("arbitrary", "arbitrary")),
        name="mm_res",
    )(a, w, x, mods)


def _sg_proj_kernel(h_ref, w_ref, b_ref, o_ref):
    z = _dot(h_ref[...], w_ref[...]) + b_ref[...]
    o_ref[...] = 0.5 * z * (1.0 + lax.erf(z * (2.0 ** -0.5)))


def _sg_proj(h, w, b):
    t, d = h.shape
    n2 = w.shape[1]
    tm, tn = TM, 512
    return pl.pallas_call(
        _sg_proj_kernel,
        grid=(t // tm, n2 // tn),
        in_specs=[
            pl.BlockSpec((tm, d), lambda m, n: (m, 0)),
            pl.BlockSpec((d, tn), lambda m, n: (0, n)),
            pl.BlockSpec((1, tn), lambda m, n: (0, n)),
        ],
        out_specs=pl.BlockSpec((tm, tn), lambda m, n: (m, n)),
        out_shape=jax.ShapeDtypeStruct((t, n2), F32),
        compiler_params=_params(("arbitrary", "arbitrary")),
        name="sg_proj",
    )(h, w, b.reshape(1, n2))


def _layernorm_kernel(v_ref, g_ref, b_ref, o_ref):
    v = v_ref[...]
    vc = v - jnp.mean(v, axis=-1, keepdims=True)
    var = jnp.mean(vc * vc, axis=-1, keepdims=True)
    o_ref[...] = vc * lax.rsqrt(var + EPS) * g_ref[...] + b_ref[...]


def _layernorm_v(z, g, b):
    t, n2 = z.shape
    d = n2 // 2
    tm = TM_NM
    return pl.pallas_call(
        _layernorm_kernel,
        grid=(t // tm,),
        in_specs=[
            pl.BlockSpec((tm, d), lambda m: (m, 1)),
            pl.BlockSpec((1, d), lambda m: (0, 0)),
            pl.BlockSpec((1, d), lambda m: (0, 0)),
        ],
        out_specs=pl.BlockSpec((tm, d), lambda m: (m, 0)),
        out_shape=jax.ShapeDtypeStruct((t, d), F32),
        compiler_params=_params(("arbitrary",)),
        name="layernorm_v",
    )(z, g.reshape(1, d), b.reshape(1, d))


def _gate_prompt_kernel(groups, u_ref, v_ref, ws_ref, bs_ref, o_ref):
    chunk = ws_ref.shape[1]
    hd = u_ref.shape[1] // groups
    row = lax.broadcasted_iota(jnp.int32, (chunk, chunk), 0)
    col = lax.broadcasted_iota(jnp.int32, (chunk, chunk), 1)
    for g in range(groups):
        ws = jnp.where(col <= row, ws_ref[g], 0.0).astype(BF16)
        vg = v_ref[:, g * hd:(g + 1) * hd].astype(BF16)
        mixed = _dot(ws, vg) + bs_ref[:, g:g + 1]
        o_ref[:, g * hd:(g + 1) * hd] = (u_ref[:, g * hd:(g + 1) * hd] * mixed).astype(o_ref.dtype)


def _gate_prompt(z, vn, w_s, b_s_t, n_prompt_rows):
    groups, chunk, _ = w_s.shape
    d = vn.shape[1]
    return pl.pallas_call(
        functools.partial(_gate_prompt_kernel, groups),
        grid=(n_prompt_rows // chunk,),
        in_specs=[
            pl.BlockSpec((chunk, d), lambda c: (c, 0)),
            pl.BlockSpec((chunk, d), lambda c: (c, 0)),
            pl.BlockSpec((groups, chunk, chunk), lambda c: (0, 0, 0)),
            pl.BlockSpec((chunk, groups), lambda c: (0, 0)),
        ],
        out_specs=pl.BlockSpec((chunk, d), lambda c: (c, 0)),
        out_shape=jax.ShapeDtypeStruct((n_prompt_rows, d), BF16),
        compiler_params=_params(("arbitrary",)),
        name="gate_prompt",
    )(z, vn, w_s, b_s_t)


def _gate_sample_kernel(n_pos, ws_ref, bs_ref, u_ref, v_ref, o_ref):
    g = pl.program_id(0)
    for t in range(n_pos):
        mixed = None
        for s in range(t + 1):
            term = ws_ref[g, t * n_pos + s] * v_ref[s * SLAB:(s + 1) * SLAB, :]
            mixed = term if mixed is None else mixed + term
        mixed = mixed + bs_ref[g, t]
        o_ref[t * SLAB:(t + 1) * SLAB, :] = (u_ref[t * SLAB:(t + 1) * SLAB, :] * mixed).astype(o_ref.dtype)


def _gate_sample(z, vn, ws_small, bs_small, n_prompt_rows, n_pos):
    groups = ws_small.shape[0]
    d = vn.shape[1]
    hd = d // groups
    rows = n_pos * SLAB
    blk = n_prompt_rows // rows
    return pl.pallas_call(
        functools.partial(_gate_sample_kernel, n_pos),
        grid=(groups,),
        in_specs=[
            pl.BlockSpec(memory_space=pltpu.SMEM),
            pl.BlockSpec(memory_space=pltpu.SMEM),
            pl.BlockSpec((rows, hd), lambda g: (blk, g)),
            pl.BlockSpec((rows, hd), lambda g: (blk, g)),
        ],
        out_specs=pl.BlockSpec((rows, hd), lambda g: (0, g)),
        out_shape=jax.ShapeDtypeStruct((rows, d), BF16),
        compiler_params=_params(("arbitrary",)),
        name="gate_sample",
    )(ws_small, bs_small, z, vn)


def _gather_kernel(total_ref, src_ref, h_hbm, o_ref, buf, sem):
    i = pl.program_id(0)
    tm = buf.shape[0]

    @pl.when(i < total_ref[0])
    def _():
        def issue(j, carry):
            r = src_ref[0, 0, j]
            pltpu.make_async_copy(h_hbm.at[pl.ds(r, 1), :], buf.at[pl.ds(j, 1), :], sem).start()
            return carry

        lax.fori_loop(0, tm, issue, 0)

        def drain(j, carry):
            pltpu.make_async_copy(h_hbm.at[pl.ds(0, 1), :], buf.at[pl.ds(j, 1), :], sem).wait()
            return carry

        lax.fori_loop(0, tm, drain, 0)
        o_ref[...] = buf[...].astype(o_ref.dtype)

    @pl.when(i >= total_ref[0])
    def _():
        o_ref[...] = jnp.zeros(o_ref.shape, o_ref.dtype)


def _gather_rows(h, src_rows, total_tiles, n_tiles):
    t, d = h.shape
    tm = TM_E
    return pl.pallas_call(
        _gather_kernel,
        grid_spec=pltpu.PrefetchScalarGridSpec(
            num_scalar_prefetch=1,
            grid=(n_tiles,),
            in_specs=[
                pl.BlockSpec((1, 1, tm), lambda i, tot: (i, 0, 0), memory_space=pltpu.SMEM),
                pl.BlockSpec(memory_space=pl.ANY),
            ],
            out_specs=pl.BlockSpec((tm, d), lambda i, tot: (i, 0)),
            scratch_shapes=[pltpu.VMEM((tm, d), F32), pltpu.SemaphoreType.DMA(())],
        ),
        out_shape=jax.ShapeDtypeStruct((n_tiles * tm, d), BF16),
        compiler_params=_params(("arbitrary",)),
        name="moe_gather",
    )(total_tiles, src_rows.reshape(n_tiles, 1, tm), h)


def _expert_in_kernel(e_ref, n_ref, m_ref, om_ref, on_ref, v_ref, x_ref, wg_ref, wu_ref, bg_ref, bu_ref, o_ref):
    s = pl.program_id(0)

    @pl.when(v_ref[s] == 1)
    def _():
        x = x_ref[...]
        g = jnp.minimum(_dot(x, wg_ref[0]) + bg_ref[0], SWIGLU_LIMIT)
        u = jnp.clip(_dot(x, wu_ref[0]) + bu_ref[0], -SWIGLU_LIMIT, SWIGLU_LIMIT)
        o_ref[...] = ((u + 1.0) * (g * jax.nn.sigmoid(SWIGLU_ALPHA * g))).astype(o_ref.dtype)

    @pl.when(v_ref[s] == 0)
    def _():
        o_ref[...] = jnp.zeros(o_ref.shape, o_ref.dtype)


def _expert_in(xs, w_gate, b_gate, w_up, b_up, items, n_tiles):
    n_e, d, de = w_gate.shape
    tm, tn = TM_E, 512
    n_items = items[0].shape[0]
    return pl.pallas_call(
        _expert_in_kernel,
        grid_spec=pltpu.PrefetchScalarGridSpec(
            num_scalar_prefetch=6,
            grid=(n_items,),
            in_specs=[
                pl.BlockSpec((tm, d), lambda s, e, n, m, om, on, v: (m[s], 0)),
                pl.BlockSpec((1, d, tn), lambda s, e, n, m, om, on, v: (e[s], 0, n[s])),
                pl.BlockSpec((1, d, tn), lambda s, e, n, m, om, on, v: (e[s], 0, n[s])),
                pl.BlockSpec((1, 1, tn), lambda s, e, n, m, om, on, v: (e[s], 0, n[s])),
                pl.BlockSpec((1, 1, tn), lambda s, e, n, m, om, on, v: (e[s], 0, n[s])),
            ],
            out_specs=pl.BlockSpec((tm, tn), lambda s, e, n, m, om, on, v: (om[s], on[s])),
        ),
        out_shape=jax.ShapeDtypeStruct((n_tiles * tm, de), BF16),
        compiler_params=_params(("arbitrary",)),
        name="expert_in",
    )(*items, xs, w_gate, w_up, b_gate.reshape(n_e, 1, de), b_up.reshape(n_e, 1, de))


def _expert_out_kernel(e_ref, n_ref, m_ref, om_ref, on_ref, v_ref, a_ref, wd_ref, bd_ref, o_ref):
    s = pl.program_id(0)

    @pl.when(v_ref[s] == 1)
    def _():
        o_ref[...] = _dot(a_ref[...], wd_ref[0]) + bd_ref[0]

    @pl.when(v_ref[s] == 0)
    def _():
        o_ref[...] = jnp.zeros(o_ref.shape, o_ref.dtype)


def _expert_out(a, w_down, b_down, items, n_tiles):
    n_e, de, d = w_down.shape
    tm, tn = TM_E, 1024
    n_items = items[0].shape[0]
    return pl.pallas_call(
        _expert_out_kernel,
        grid_spec=pltpu.PrefetchScalarGridSpec(
            num_scalar_prefetch=6,
            grid=(n_items,),
            in_specs=[
                pl.BlockSpec((tm, de), lambda s, e, n, m, om, on, v: (m[s], 0)),
                pl.BlockSpec((1, de, tn), lambda s, e, n, m, om, on, v: (e[s], 0, n[s])),
                pl.BlockSpec((1, 1, tn), lambda s, e, n, m, om, on, v: (e[s], 0, n[s])),
            ],
            out_specs=pl.BlockSpec((tm, tn), lambda s, e, n, m, om, on, v: (om[s], on[s])),
        ),
        out_shape=jax.ShapeDtypeStruct((n_tiles * tm, d), F32),
        compiler_params=_params(("arbitrary",)),
        name="expert_out",
    )(*items, a, w_down, b_down.reshape(n_e, 1, d))


def _combine_kernel(pos_ref, x_ref, gate_ref, p_ref, y_hbm, o_ref, buf, sem):
    tm = x_ref.shape[0]

    def issue(j, carry):
        for k in range(TOP_K):
            r = pos_ref[0, 0, j * TOP_K + k]
            pltpu.make_async_copy(y_hbm.at[pl.ds(r, 1), :], buf.at[k, pl.ds(j, 1), :], sem).start()
        return carry

    lax.fori_loop(0, tm, issue, 0)

    def drain(j, carry):
        for k in range(TOP_K):
            pltpu.make_async_copy(y_hbm.at[pl.ds(0, 1), :], buf.at[k, pl.ds(j, 1), :], sem).wait()
        return carry

    lax.fori_loop(0, tm, drain, 0)
    f = p_ref[:, 0:1] * buf[0]
    for k in range(1, TOP_K):
        f = f + p_ref[:, k:k + 1] * buf[k]
    o_ref[...] = x_ref[...] + gate_ref[...] * f


def _combine(x, y, pos, probs, mods, k_gate, n_prompt_rows, seq_len):
    t, d = x.shape
    tm = TM_C
    rb = _mod_rowblock(tm, n_prompt_rows, seq_len)
    return pl.pallas_call(
        _combine_kernel,
        grid=(t // tm,),
        in_specs=[
            pl.BlockSpec((1, 1, tm * TOP_K), lambda m: (m, 0, 0), memory_space=pltpu.SMEM),
            pl.BlockSpec((tm, d), lambda m: (m, 0)),
            pl.BlockSpec((SLAB, d), lambda m: (rb(m), k_gate)),
            pl.BlockSpec((tm, LANES), lambda m: (m, 0)),
            pl.BlockSpec(memory_space=pl.ANY),
        ],
        out_specs=pl.BlockSpec((tm, d), lambda m: (m, 0)),
        out_shape=jax.ShapeDtypeStruct((t, d), F32),
        scratch_shapes=[pltpu.VMEM((TOP_K, tm, d), F32), pltpu.SemaphoreType.DMA(())],
        compiler_params=_params(("arbitrary",)),
        name="moe_combine",
    )(pos.reshape(t // tm, 1, tm * TOP_K), x, mods, probs, y)


def _routing_tables(top_idx, n_experts, n_tiles, nb_in, nb_out):
    t = top_idx.shape[0]
    e_flat = top_idx.reshape(-1)
    onehot = (e_flat[:, None] == jnp.arange(n_experts, dtype=jnp.int32)[None, :]).astype(jnp.int32)
    csum = jnp.cumsum(onehot, axis=0)
    counts = csum[-1]
    rank = jnp.take_along_axis(csum, e_flat[:, None], axis=1)[:, 0] - 1
    tiles_e = (counts + TM_E - 1) // TM_E
    tile_end = jnp.cumsum(tiles_e)
    tile_start = tile_end - tiles_e
    total_tiles = tile_end[-1]
    pos = tile_start[e_flat] * TM_E + rank
    src_rows = jnp.zeros((n_tiles * TM_E,), jnp.int32).at[pos].set(
        jnp.arange(t * TOP_K, dtype=jnp.int32) // TOP_K)

    def items(nb):
        per_e = tiles_e * nb
        item_end = jnp.cumsum(per_e)
        item_start = item_end - per_e
        total = item_end[-1]
        s = jnp.arange(n_tiles * nb, dtype=jnp.int32)
        valid = s < total
        sc = jnp.minimum(s, jnp.maximum(total - 1, 0))
        e = jnp.minimum(jnp.searchsorted(item_end, sc, side="right").astype(jnp.int32), n_experts - 1)
        local = sc - item_start[e]
        ce = jnp.maximum(tiles_e[e], 1)
        n_in = (local // ce).astype(jnp.int32)
        m_in = (tile_start[e] + local % ce).astype(jnp.int32)
        spare = jnp.maximum(s - total, 0)
        m_out = jnp.where(valid, m_in, total_tiles + spare // nb).astype(jnp.int32)
        n_out = jnp.where(valid, n_in, spare % nb).astype(jnp.int32)
        return (e, n_in, m_in, m_out, n_out, valid.astype(jnp.int32))

    return pos.astype(jnp.int32), src_rows, total_tiles.reshape(1).astype(jnp.int32), items(nb_in), items(nb_out)


def _moe(x, g, mods, rb, n_prompt_rows, seq_len, w_router, b_router, w_gate, b_gate, w_up, b_up, w_down, b_down):
    t, d = x.shape
    n_experts = w_router.shape[1]
    n_tiles = t * TOP_K // TM_E + n_experts
    h, idx, probs = _normmod_router(x, g, mods, 3, 4, rb, w_router, b_router)
    pos, src_rows, total_tiles, items_in, items_out = _routing_tables(
        idx[:, :TOP_K], n_experts, n_tiles, w_gate.shape[2] // 512, d // 1024)
    xs = _gather_rows(h, src_rows, total_tiles, n_tiles)
    a = _expert_in(xs, w_gate, b_gate, w_up, b_up, items_in, n_tiles)
    y = _expert_out(a, w_down, b_down, items_out, n_tiles)
    return _combine(x, y, pos, probs, mods, 5, n_prompt_rows, seq_len)


def kernel(x_prompt, x_sample, c_prompt, c_sample, state_conv, norm_mix_g, norm_ffn_g, ada_w, ada_b, conv_w_in, conv_w, conv_w_out, sg_w_proj, sg_b_proj, sg_ln_g, sg_ln_b, sg_w_s, sg_b_s, sg_w_out, router_w, router_b, exp_w_gate, exp_b_gate, exp_w_up, exp_b_up, exp_w_down, exp_b_down, final_norm_g):
    bp, sp, d = x_prompt.shape
    bs, ss, _ = x_sample.shape
    depth = norm_mix_g.shape[0]
    n_prompt_rows = bp * sp
    n_sample_rows = bs * ss
    assert bs == SLAB and n_sample_rows == TM and sp % TM == 0 and sg_w_s.shape[2] == SLAB

    x = jnp.concatenate([x_prompt.reshape(n_prompt_rows, d),
                         x_sample.transpose(1, 0, 2).reshape(n_sample_rows, d)], axis=0)

    n_c = bp + bs
    c_all = jnp.pad(jnp.concatenate([c_prompt, c_sample], axis=0), ((0, -n_c % 8), (0, 0)))
    mods_c = _ada(c_all, ada_w.astype(BF16), ada_b)
    mods_all = jnp.concatenate(
        [jnp.repeat(mods_c[:, :bp], SLAB, axis=1), mods_c[:, bp:n_c]], axis=1)

    rb_nm = _mod_rowblock(TM_NM, n_prompt_rows, sp)
    rb_mm = _mod_rowblock(TM, n_prompt_rows, sp)

    conv_new_p, conv_new_s, chunk_new_s = [], [], []
    for i in range(depth):
        mods = mods_all[i]
        j = i // 2
        h = _normmod(x, norm_mix_g[i], mods, 0, 1, rb_nm, BF16)
        if i % 2 == 0:
            state = state_conv[j].transpose(1, 0, 2)
            u, cz = _conv_in(h, conv_w_in[j].astype(BF16), conv_w[j], state, n_prompt_rows, sp)
            x = _mm_res(u, conv_w_out[j].astype(BF16), x, mods, 2, rb_mm)
            conv_new_p.append(cz[:n_prompt_rows].reshape(bp, sp, d)[:, sp - 2:])
            conv_new_s.append(cz[n_prompt_rows:].reshape(ss, bs, d)[ss - 2:].transpose(1, 0, 2))
        else:
            z = _sg_proj(h, sg_w_proj[j].astype(BF16), sg_b_proj[j])
            vn = _layernorm_v(z, sg_ln_g[j], sg_ln_b[j])
            gp = _gate_prompt(z, vn, sg_w_s[j], sg_b_s[j].T, n_prompt_rows)
            ws_small = sg_w_s[j][:, :ss, :ss].reshape(-1, ss * ss)
            gs = _gate_sample(z, vn, ws_small, sg_b_s[j][:, :ss], n_prompt_rows, ss)
            gated = jnp.concatenate([gp, gs], axis=0)
            x = _mm_res(gated, sg_w_out[j].astype(BF16), x, mods, 2, rb_mm)
            chunk_new_s.append(vn[n_prompt_rows:].reshape(ss, bs, d).transpose(1, 0, 2))
        x = _moe(x, norm_ffn_g[i], mods, rb_nm, n_prompt_rows, sp, router_w[i], router_b[i],
                 exp_w_gate[i].astype(BF16), exp_b_gate[i], exp_w_up[i].astype(BF16), exp_b_up[i],
                 exp_w_down[i].astype(BF16), exp_b_down[i])

    y = _final_norm(x, final_norm_g)
    y_prompt = y[:n_prompt_rows].reshape(bp, sp, d)
    y_sample = y[n_prompt_rows:].reshape(ss, bs, d).transpose(1, 0, 2)
    return (y_prompt, y_sample, jnp.stack(conv_new_p), jnp.stack(conv_new_s), jnp.stack(chunk_new_s))
```

```python
import functools

import jax
import jax.numpy as jnp
from jax import lax
from jax.experimental import pallas as pl
from jax.experimental.pallas import tpu as pltpu

F32 = jnp.float32
BF16 = jnp.bfloat16

EPS = 1e-6
TOP_K = 4
SWIGLU_LIMIT = 7.0
SWIGLU_ALPHA = 1.702
N_MOD = 6
LANES = 128
SLAB = 128
VMEM_LIMIT = 56 * 1024 * 1024

TM = 1024
TM_E = 256
TM_NM = 256
TM_C = 128


def _params(sem, vmem=VMEM_LIMIT):
    return pltpu.CompilerParams(dimension_semantics=sem, vmem_limit_bytes=vmem)


def _dot(a, b):
    return jnp.dot(a, b, preferred_element_type=F32)


def _slab_bcast(y, row_fn):
    tm, n = y.shape
    return row_fn(y.reshape(tm // SLAB, SLAB, n)).reshape(tm, n)


def _ada_kernel(c_ref, w_ref, b_ref, o_ref):
    c = c_ref[...]
    s = (c * jax.nn.sigmoid(c)).astype(BF16)
    o_ref[0] = _dot(s, w_ref[0].astype(BF16)) + b_ref[0]


def _ada(c_all, ada_w, ada_b):
    depth, d, n6 = ada_w.shape
    rows = c_all.shape[0]
    tn = 512
    return pl.pallas_call(
        _ada_kernel,
        grid=(depth, n6 // tn),
        in_specs=[
            pl.BlockSpec((rows, d), lambda l, n: (0, 0)),
            pl.BlockSpec((1, d, tn), lambda l, n: (l, 0, n)),
            pl.BlockSpec((1, 1, tn), lambda l, n: (l, 0, n)),
        ],
        out_specs=pl.BlockSpec((1, rows, tn), lambda l, n: (l, 0, n)),
        out_shape=jax.ShapeDtypeStruct((depth, rows, n6), F32),
        compiler_params=_params(("arbitrary", "arbitrary")),
        name="ada",
    )(c_all, ada_w, ada_b.reshape(depth, 1, n6))


def _rms(x, g):
    return x * lax.rsqrt(jnp.mean(x * x, axis=-1, keepdims=True) + EPS) * g


def _normmod_kernel(x_ref, g_ref, sh_ref, sc_ref, o_ref):
    y = _rms(x_ref[...], g_ref[...])
    h = _slab_bcast(y, lambda y3: y3 * (1.0 + sc_ref[...])[None] + sh_ref[...][None])
    o_ref[...] = h.astype(o_ref.dtype)


def _mod_rowblock(tm, n_prompt_rows, seq_len):
    n_prompt_tiles = n_prompt_rows // tm
    tiles_per_seq = seq_len // tm
    n_seq = n_prompt_rows // seq_len
    return lambda m: jnp.where(m < n_prompt_tiles, m // tiles_per_seq, n_seq)


def _normmod(x, g, mods, k_shift, k_scale, rb, out_dtype):
    t, d = x.shape
    tm = TM_NM
    return pl.pallas_call(
        _normmod_kernel,
        grid=(t // tm,),
        in_specs=[
            pl.BlockSpec((tm, d), lambda m: (m, 0)),
            pl.BlockSpec((1, d), lambda m: (0, 0)),
            pl.BlockSpec((SLAB, d), lambda m: (rb(m), k_shift)),
            pl.BlockSpec((SLAB, d), lambda m: (rb(m), k_scale)),
        ],
        out_specs=pl.BlockSpec((tm, d), lambda m: (m, 0)),
        out_shape=jax.ShapeDtypeStruct((t, d), out_dtype),
        compiler_params=_params(("arbitrary",)),
        name="normmod",
    )(x, g.reshape(1, d), mods, mods)


def _normmod_router_kernel(n_experts, x_ref, g_ref, sh_ref, sc_ref, wr_ref, br_ref,
                           h_ref, idx_ref, p_ref, rank_ref, cnt_ref, base_ref):
    @pl.when(pl.program_id(0) == 0)
    def _():
        base_ref[...] = jnp.zeros(base_ref.shape, F32)

    y = _rms(x_ref[...], g_ref[...])
    h = _slab_bcast(y, lambda y3: y3 * (1.0 + sc_ref[...])[None] + sh_ref[...][None])
    h_ref[...] = h
    logits = jnp.dot(h, wr_ref[...], preferred_element_type=F32, precision=lax.Precision.HIGHEST) + br_ref[...]
    lane = lax.broadcasted_iota(jnp.int32, logits.shape, 1)
    neg = jnp.float32(-jnp.inf)
    l = jnp.where(lane < n_experts, logits, neg)
    vals, idxs = [], []
    for _ in range(TOP_K):
        mx = jnp.max(l, axis=-1, keepdims=True)
        ix = jnp.min(jnp.where(l == mx, lane, LANES), axis=-1, keepdims=True)
        vals.append(mx)
        idxs.append(ix)
        l = jnp.where(lane == ix, neg, l)
    es = [jnp.exp(v - vals[0]) for v in vals]
    tot = es[0] + es[1] + es[2] + es[3]
    idx_out = jnp.zeros(logits.shape, jnp.int32)
    p_out = jnp.zeros(logits.shape, F32)
    for k in range(TOP_K):
        idx_out = jnp.where(lane == k, idxs[k], idx_out)
        p_out = jnp.where(lane == k, es[k] / tot, p_out)
    idx_ref[...] = idx_out
    p_ref[...] = p_out

    tm = logits.shape[0]
    chosen = jnp.zeros(logits.shape, F32)
    for k in range(TOP_K):
        chosen = chosen + (lane == idxs[k]).astype(F32)
    row = lax.broadcasted_iota(jnp.int32, (tm, tm), 0)
    col = lax.broadcasted_iota(jnp.int32, (tm, tm), 1)
    before = _dot((col < row).astype(BF16), chosen.astype(BF16)) + base_ref[...]
    rank_out = jnp.zeros(logits.shape, jnp.int32)
    for k in range(TOP_K):
        r = jnp.sum(jnp.where(lane == idxs[k], before, 0.0), axis=-1, keepdims=True)
        rank_out = jnp.where(lane == k, r.astype(jnp.int32), rank_out)
    rank_ref[...] = rank_out
    base = base_ref[...] + jnp.sum(chosen, axis=0, keepdims=True)
    base_ref[...] = base
    cnt_ref[...] = jnp.broadcast_to(base, cnt_ref.shape).astype(jnp.int32)


def _normmod_router(x, g, mods, k_shift, k_scale, rb, w_router, b_router):
    t, d = x.shape
    n_experts = w_router.shape[1]
    tm = TM_NM
    wr = jnp.pad(w_router, ((0, 0), (0, LANES - n_experts)))
    br = jnp.pad(b_router, (0, LANES - n_experts)).reshape(1, LANES)
    return pl.pallas_call(
        functools.partial(_normmod_router_kernel, n_experts),
        grid=(t // tm,),
        in_specs=[
            pl.BlockSpec((tm, d), lambda m: (m, 0)),
            pl.BlockSpec((1, d), lambda m: (0, 0)),
            pl.BlockSpec((SLAB, d), lambda m: (rb(m), k_shift)),
            pl.BlockSpec((SLAB, d), lambda m: (rb(m), k_scale)),
            pl.BlockSpec((d, LANES), lambda m: (0, 0)),
            pl.BlockSpec((1, LANES), lambda m: (0, 0)),
        ],
        out_specs=[
            pl.BlockSpec((tm, d), lambda m: (m, 0)),
            pl.BlockSpec((tm, LANES), lambda m: (m, 0)),
            pl.BlockSpec((tm, LANES), lambda m: (m, 0)),
            pl.BlockSpec((tm, LANES), lambda m: (m, 0)),
            pl.BlockSpec((8, LANES), lambda m: (0, 0)),
        ],
        out_shape=[
            jax.ShapeDtypeStruct((t, d), F32),
            jax.ShapeDtypeStruct((t, LANES), jnp.int32),
            jax.ShapeDtypeStruct((t, LANES), F32),
            jax.ShapeDtypeStruct((t, LANES), jnp.int32),
            jax.ShapeDtypeStruct((8, LANES), jnp.int32),
        ],
        scratch_shapes=[pltpu.VMEM((1, LANES), F32)],
        compiler_params=_params(("arbitrary",)),
        name="normmod_router",
    )(x, g.reshape(1, d), mods, mods, wr, br)


def _final_norm_kernel(x_ref, g_ref, o_ref):
    o_ref[...] = _rms(x_ref[...], g_ref[...])


def _final_norm(x, g):
    t, d = x.shape
    tm = TM_NM
    return pl.pallas_call(
        _final_norm_kernel,
        grid=(t // tm,),
        in_specs=[pl.BlockSpec((tm, d), lambda m: (m, 0)), pl.BlockSpec((1, d), lambda m: (0, 0))],
        out_specs=pl.BlockSpec((tm, d), lambda m: (m, 0)),
        out_shape=jax.ShapeDtypeStruct((t, d), F32),
        compiler_params=_params(("arbitrary",)),
        name="final_norm",
    )(x, g.reshape(1, d))


def _conv_in_kernel(n_prompt_tiles, tiles_per_seq, h_ref, wb_ref, wc_ref, wz_ref, cw_ref, st_ref,
                    u_ref, cz_ref, carry_ref):
    m = pl.program_id(0)
    n = pl.program_id(1)
    h = h_ref[...]
    b = _dot(h, wb_ref[...])
    cz = _dot(h, wc_ref[...]) * _dot(h, wz_ref[...])
    cz_ref[...] = cz
    tm, tn = cz.shape
    w0 = cw_ref[0:1, :]
    w1 = cw_ref[1:2, :]
    w2 = cw_ref[2:3, :]

    @pl.when(m < n_prompt_tiles)
    def _():
        @pl.when(m % tiles_per_seq == 0)
        def _():
            carry_ref[n] = jnp.zeros((8, tn), F32)

        prev = carry_ref[n]
        row = lax.broadcasted_iota(jnp.int32, (tm, tn), 0)
        p1 = jnp.where(row == 0, prev[7:8, :], pltpu.roll(cz, 1, 0))
        p2 = jnp.where(row == 0, prev[6:7, :], jnp.where(row == 1, prev[7:8, :], pltpu.roll(cz, 2, 0)))
        u_ref[...] = (b * (w0 * p2 + w1 * p1 + w2 * cz)).astype(u_ref.dtype)
        carry_ref[n] = cz[tm - 8:, :]

    @pl.when(m >= n_prompt_tiles)
    def _():
        st0 = st_ref[0]
        st1 = st_ref[1]
        p1 = jnp.concatenate([st1, cz[:tm - SLAB, :]], axis=0)
        p2 = jnp.concatenate([st0, st1, cz[:tm - 2 * SLAB, :]], axis=0)
        u_ref[...] = (b * (w0 * p2 + w1 * p1 + w2 * cz)).astype(u_ref.dtype)


def _conv_in(h, w_in, conv_w, state, n_prompt_rows, seq_len):
    t, d = h.shape
    tm, tn = TM, 256
    nb = d // tn
    kern = functools.partial(_conv_in_kernel, n_prompt_rows // tm, seq_len // tm)
    return pl.pallas_call(
        kern,
        grid=(t // tm, nb),
        in_specs=[
            pl.BlockSpec((tm, d), lambda m, n: (m, 0)),
            pl.BlockSpec((d, tn), lambda m, n: (0, n)),
            pl.BlockSpec((d, tn), lambda m, n: (0, n + nb)),
            pl.BlockSpec((d, tn), lambda m, n: (0, n + 2 * nb)),
            pl.BlockSpec((3, tn), lambda m, n: (0, n)),
            pl.BlockSpec((2, SLAB, tn), lambda m, n: (0, 0, n)),
        ],
        out_specs=[
            pl.BlockSpec((tm, tn), lambda m, n: (m, n)),
            pl.BlockSpec((tm, tn), lambda m, n: (m, n)),
        ],
        out_shape=[jax.ShapeDtypeStruct((t, d), BF16), jax.ShapeDtypeStruct((t, d), F32)],
        scratch_shapes=[pltpu.VMEM((nb, 8, tn), F32)],
        compiler_params=_params(("arbitrary", "arbitrary")),
        name="conv_in",
    )(h, w_in, w_in, w_in, conv_w, state)


def _mm_res_kernel(a_ref, w_ref, x_ref, gate_ref, o_ref):
    acc = _dot(a_ref[...], w_ref[...])
    o_ref[...] = x_ref[...] + _slab_bcast(acc, lambda a3: a3 * gate_ref[...][None])


def _mm_res(a, w, x, mods, k_gate, rb):
    t, d = x.shape
    kdim = a.shape[1]
    tm, tn = TM, 512
    nb = d // tn
    return pl.pallas_call(
        _mm_res_kernel,
        grid=(t // tm, nb),
        in_specs=[
            pl.BlockSpec((tm, kdim), lambda m, n: (m, 0)),
            pl.BlockSpec((kdim, tn), lambda m, n: (0, n)),
            pl.BlockSpec((tm, tn), lambda m, n: (m, n)),
            pl.BlockSpec((SLAB, tn), lambda m, n: (rb(m), k_gate * nb + n)),
        ],
        out_specs=pl.BlockSpec((tm, tn), lambda m, n: (m, n)),
        out_shape=jax.ShapeDtypeStruct((t, d), F32),
        compiler_params=_params(("arbitrary", "arbitrary")),
        name="mm_res",
    )(a, w, x, mods)


def _sg_proj_kernel(h_ref, w_ref, b_ref, o_ref):
    z = _dot(h_ref[...], w_ref[...]) + b_ref[...]
    o_ref[...] = 0.5 * z * (1.0 + lax.erf(z * (2.0 ** -0.5)))


def _sg_proj(h, w, b):
    t, d = h.shape
    n2 = w.shape[1]
    tm, tn = TM, 512
    return pl.pallas_call(
        _sg_proj_kernel,
        grid=(t // tm, n2 // tn),
        in_specs=[
            pl.BlockSpec((tm, d), lambda m, n: (m, 0)),
            pl.BlockSpec((d, tn), lambda m, n: (0, n)),
            pl.BlockSpec((1, tn), lambda m, n: (0, n)),
        ],
        out_specs=pl.BlockSpec((tm, tn), lambda m, n: (m, n)),
        out_shape=jax.ShapeDtypeStruct((t, n2), F32),
        compiler_params=_params(("arbitrary", "arbitrary")),
        name="sg_proj",
    )(h, w, b.reshape(1, n2))


def _layernorm_kernel(v_ref, g_ref, b_ref, o_ref):
    v = v_ref[...]
    vc = v - jnp.mean(v, axis=-1, keepdims=True)
    var = jnp.mean(vc * vc, axis=-1, keepdims=True)
    o_ref[...] = vc * lax.rsqrt(var + EPS) * g_ref[...] + b_ref[...]


def _layernorm_v(z, g, b):
    t, n2 = z.shape
    d = n2 // 2
    tm = TM_NM
    return pl.pallas_call(
        _layernorm_kernel,
        grid=(t // tm,),
        in_specs=[
            pl.BlockSpec((tm, d), lambda m: (m, 1)),
            pl.BlockSpec((1, d), lambda m: (0, 0)),
            pl.BlockSpec((1, d), lambda m: (0, 0)),
        ],
        out_specs=pl.BlockSpec((tm, d), lambda m: (m, 0)),
        out_shape=jax.ShapeDtypeStruct((t, d), F32),
        compiler_params=_params(("arbitrary",)),
        name="layernorm_v",
    )(z, g.reshape(1, d), b.reshape(1, d))


def _gate_prompt_kernel(groups, u_ref, v_ref, ws_ref, bs_ref, o_ref):
    chunk = ws_ref.shape[1]
    hd = u_ref.shape[1] // groups
    row = lax.broadcasted_iota(jnp.int32, (chunk, chunk), 0)
    col = lax.broadcasted_iota(jnp.int32, (chunk, chunk), 1)
    for g in range(groups):
        ws = jnp.where(col <= row, ws_ref[g], 0.0).astype(BF16)
        vg = v_ref[:, g * hd:(g + 1) * hd].astype(BF16)
        mixed = _dot(ws, vg) + bs_ref[:, g:g + 1]
        o_ref[:, g * hd:(g + 1) * hd] = (u_ref[:, g * hd:(g + 1) * hd] * mixed).astype(o_ref.dtype)


def _gate_prompt(z, vn, w_s, b_s_t, n_prompt_rows):
    groups, chunk, _ = w_s.shape
    d = vn.shape[1]
    return pl.pallas_call(
        functools.partial(_gate_prompt_kernel, groups),
        grid=(n_prompt_rows // chunk,),
        in_specs=[
            pl.BlockSpec((chunk, d), lambda c: (c, 0)),
            pl.BlockSpec((chunk, d), lambda c: (c, 0)),
            pl.BlockSpec((groups, chunk, chunk), lambda c: (0, 0, 0)),
            pl.BlockSpec((chunk, groups), lambda c: (0, 0)),
        ],
        out_specs=pl.BlockSpec((chunk, d), lambda c: (c, 0)),
        out_shape=jax.ShapeDtypeStruct((n_prompt_rows, d), BF16),
        compiler_params=_params(("arbitrary",)),
        name="gate_prompt",
    )(z, vn, w_s, b_s_t)


def _gate_sample_kernel(n_pos, ws_ref, bs_ref, u_ref, v_ref, o_ref):
    g = pl.program_id(0)
    for t in range(n_pos):
        mixed = None
        for s in range(t + 1):
            term = ws_ref[g, t * n_pos + s] * v_ref[s * SLAB:(s + 1) * SLAB, :]
            mixed = term if mixed is None else mixed + term
        mixed = mixed + bs_ref[g, t]
        o_ref[t * SLAB:(t + 1) * SLAB, :] = (u_ref[t * SLAB:(t + 1) * SLAB, :] * mixed).astype(o_ref.dtype)


def _gate_sample(z, vn, ws_small, bs_small, n_prompt_rows, n_pos):
    groups = ws_small.shape[0]
    d = vn.shape[1]
    hd = d // groups
    rows = n_pos * SLAB
    blk = n_prompt_rows // rows
    return pl.pallas_call(
        functools.partial(_gate_sample_kernel, n_pos),
        grid=(groups,),
        in_specs=[
            pl.BlockSpec(memory_space=pltpu.SMEM),
            pl.BlockSpec(memory_space=pltpu.SMEM),
            pl.BlockSpec((rows, hd), lambda g: (blk, g)),
            pl.BlockSpec((rows, hd), lambda g: (blk, g)),
        ],
        out_specs=pl.BlockSpec((rows, hd), lambda g: (0, g)),
        out_shape=jax.ShapeDtypeStruct((rows, d), BF16),
        compiler_params=_params(("arbitrary",)),
        name="gate_sample",
    )(ws_small, bs_small, z, vn)


def _gather_kernel(total_ref, src_ref, nxt_ref, h_hbm, o_ref, buf, sem):
    i = pl.program_id(0)
    tm = buf.shape[1]
    total = total_ref[0]
    slot = i % 2

    def start(rows_ref, s):
        def issue(j, carry):
            r = rows_ref[0, 0, j]
            pltpu.make_async_copy(h_hbm.at[pl.ds(r, 1), :], buf.at[s, pl.ds(j, 1), :], sem.at[s]).start()
            return carry

        lax.fori_loop(0, tm, issue, 0, unroll=8)

    @pl.when(i == 0)
    def _():
        start(src_ref, 0)

    @pl.when(i + 1 < total)
    def _():
        start(nxt_ref, 1 - slot)

    @pl.when(i < total)
    def _():
        pltpu.make_async_copy(h_hbm.at[pl.ds(0, tm), :], buf.at[slot], sem.at[slot]).wait()
        o_ref[...] = buf[slot].astype(o_ref.dtype)

    @pl.when(i >= total)
    def _():
        o_ref[...] = jnp.zeros(o_ref.shape, o_ref.dtype)


def _gather_rows(h, src_rows, total_tiles, n_tiles):
    t, d = h.shape
    tm = TM_E
    return pl.pallas_call(
        _gather_kernel,
        grid_spec=pltpu.PrefetchScalarGridSpec(
            num_scalar_prefetch=1,
            grid=(n_tiles,),
            in_specs=[
                pl.BlockSpec((1, 1, tm), lambda i, tot: (i, 0, 0), memory_space=pltpu.SMEM),
                pl.BlockSpec((1, 1, tm), lambda i, tot: (jnp.minimum(i + 1, n_tiles - 1), 0, 0),
                             memory_space=pltpu.SMEM),
                pl.BlockSpec(memory_space=pl.ANY),
            ],
            out_specs=pl.BlockSpec((tm, d), lambda i, tot: (i, 0)),
            scratch_shapes=[pltpu.VMEM((2, tm, d), F32), pltpu.SemaphoreType.DMA((2,))],
        ),
        out_shape=jax.ShapeDtypeStruct((n_tiles * tm, d), BF16),
        compiler_params=_params(("arbitrary",)),
        name="moe_gather",
    )(total_tiles, src_rows.reshape(n_tiles, 1, tm), src_rows.reshape(n_tiles, 1, tm), h)


(F_E, F_N, F_M, F_OM, F_ON, F_VALID, F_FIRST, F_SLOT, F_HAS_NEXT, F_NEXT_E, F_NEXT_N, N_FIELDS) = range(12)
K_CHUNK = 512


def _grouped_matmul_kernel(layer, n_items, n_w, epilogue, tab, x_ref, *refs):
    w_hbm = refs[:n_w]
    b_refs = refs[n_w:2 * n_w]
    o_ref = refs[2 * n_w]
    w_buf = refs[2 * n_w + 1:3 * n_w + 1]
    w_bf = refs[3 * n_w + 1:4 * n_w + 1]
    sem = refs[4 * n_w + 1]
    s = pl.program_id(0)
    fld = lambda f: tab[f * n_items + s]
    kdim, tn = w_bf[0].shape
    valid = fld(F_VALID)
    first = fld(F_FIRST)
    slot = fld(F_SLOT)

    def copies(e, n, sl):
        col = pl.multiple_of(n * tn, tn)
        return [pltpu.make_async_copy(w_hbm[i].at[layer, e, :, pl.ds(col, tn)], w_buf[i].at[sl], sem.at[i, sl])
                for i in range(n_w)]

    @pl.when(s == 0)
    def _():
        for c in copies(fld(F_E), fld(F_N), 0):
            c.start()

    @pl.when((valid == 1) & (first == 1))
    def _():
        for c in copies(0, 0, slot):
            c.wait()

        @pl.when(fld(F_HAS_NEXT) == 1)
        def _():
            for c in copies(fld(F_NEXT_E), fld(F_NEXT_N), 1 - slot):
                c.start()

        acc = [None] * n_w
        for c in range(kdim // K_CHUNK):
            ks = slice(c * K_CHUNK, (c + 1) * K_CHUNK)
            xk = x_ref[:, ks]
            for i in range(n_w):
                w = w_buf[i][slot, ks, :].astype(BF16)
                w_bf[i][ks, :] = w
                p = _dot(xk, w)
                acc[i] = p if acc[i] is None else acc[i] + p
        o_ref[...] = epilogue(*[acc[i] + b_refs[i][0] for i in range(n_w)]).astype(o_ref.dtype)

    @pl.when((valid == 1) & (first == 0))
    def _():
        x = x_ref[...]
        o_ref[...] = epilogue(*[_dot(x, w_bf[i][...]) + b_refs[i][0] for i in range(n_w)]).astype(o_ref.dtype)

    @pl.when(valid == 0)
    def _():
        o_ref[...] = jnp.zeros(o_ref.shape, o_ref.dtype)


def _swiglu(g, u):
    g = jnp.minimum(g, SWIGLU_LIMIT)
    u = jnp.clip(u, -SWIGLU_LIMIT, SWIGLU_LIMIT)
    return (u + 1.0) * (g * jax.nn.sigmoid(SWIGLU_ALPHA * g))


def _grouped_matmul(x, weights, biases, layer, tab, n_tiles, tn, epilogue, out_dtype, name):
    depth, n_e, kdim, n_out = weights[0].shape
    n_w = len(weights)
    tm = TM_E
    n_items = n_tiles * (n_out // tn)
    bias_spec = pl.BlockSpec((1, 1, tn), lambda s, t: (layer * n_e + t[F_E * n_items + s], 0, t[F_N * n_items + s]))
    return pl.pallas_call(
        functools.partial(_grouped_matmul_kernel, layer, n_items, n_w, epilogue),
        grid_spec=pltpu.PrefetchScalarGridSpec(
            num_scalar_prefetch=1,
            grid=(n_items,),
            in_specs=[pl.BlockSpec((tm, kdim), lambda s, t: (t[F_M * n_items + s], 0))]
            + [pl.BlockSpec(memory_space=pl.ANY)] * n_w
            + [bias_spec] * n_w,
            out_specs=pl.BlockSpec((tm, tn), lambda s, t: (t[F_OM * n_items + s], t[F_ON * n_items + s])),
            scratch_shapes=[pltpu.VMEM((2, kdim, tn), F32)] * n_w
            + [pltpu.VMEM((kdim, tn), BF16)] * n_w
            + [pltpu.SemaphoreType.DMA((n_w, 2))],
        ),
        out_shape=jax.ShapeDtypeStruct((n_tiles * tm, n_out), out_dtype),
        compiler_params=_params(("arbitrary",)),
        name=name,
    )(tab, x, *weights, *[b.reshape(depth * n_e, 1, n_out) for b in biases])


def _combine_kernel(pos_ref, nxt_ref, x_ref, gate_ref, p_ref, y_hbm, o_ref, buf, sem):
    i = pl.program_id(0)
    tm = x_ref.shape[0]
    slot = i % 2

    def start(rows_ref, s):
        def issue(j, carry):
            for k in range(TOP_K):
                r = rows_ref[0, 0, j * TOP_K + k]
                pltpu.make_async_copy(y_hbm.at[pl.ds(r, 1), :], buf.at[s, k, pl.ds(j, 1), :], sem.at[s]).start()
            return carry

        lax.fori_loop(0, tm, issue, 0, unroll=4)

    @pl.when(i == 0)
    def _():
        start(pos_ref, 0)

    @pl.when(i + 1 < pl.num_programs(0))
    def _():
        start(nxt_ref, 1 - slot)

    for k in range(TOP_K):
        pltpu.make_async_copy(y_hbm.at[pl.ds(0, tm), :], buf.at[slot, k], sem.at[slot]).wait()
    f = p_ref[:, 0:1] * buf[slot, 0]
    for k in range(1, TOP_K):
        f = f + p_ref[:, k:k + 1] * buf[slot, k]
    o_ref[...] = x_ref[...] + gate_ref[...] * f


def _combine(x, y, pos, probs, mods, k_gate, n_prompt_rows, seq_len):
    t, d = x.shape
    tm = TM_C
    rb = _mod_rowblock(tm, n_prompt_rows, seq_len)
    return pl.pallas_call(
        _combine_kernel,
        grid=(t // tm,),
        in_specs=[
            pl.BlockSpec((1, 1, tm * TOP_K), lambda m: (m, 0, 0), memory_space=pltpu.SMEM),
            pl.BlockSpec((1, 1, tm * TOP_K), lambda m: (jnp.minimum(m + 1, t // tm - 1), 0, 0),
                         memory_space=pltpu.SMEM),
            pl.BlockSpec((tm, d), lambda m: (m, 0)),
            pl.BlockSpec((SLAB, d), lambda m: (rb(m), k_gate)),
            pl.BlockSpec((tm, LANES), lambda m: (m, 0)),
            pl.BlockSpec(memory_space=pl.ANY),
        ],
        out_specs=pl.BlockSpec((tm, d), lambda m: (m, 0)),
        out_shape=jax.ShapeDtypeStruct((t, d), F32),
        scratch_shapes=[pltpu.VMEM((2, TOP_K, tm, d), F32), pltpu.SemaphoreType.DMA((2,))],
        compiler_params=_params(("arbitrary",)),
        name="moe_combine",
    )(pos.reshape(t // tm, 1, tm * TOP_K), pos.reshape(t // tm, 1, tm * TOP_K), x, mods, probs, y)


def _routing_tables(top_idx, rank, counts, n_tiles, nb_in, nb_out):
    t = top_idx.shape[0]
    n_experts = counts.shape[0]
    ar = jnp.arange(n_experts, dtype=jnp.int32)
    tiles_e = (counts + TM_E - 1) // TM_E
    tile_end = jnp.cumsum(tiles_e)
    tile_start = tile_end - tiles_e
    total_tiles = tile_end[-1]
    group_row = jnp.sum(jnp.where(top_idx[:, :, None] == ar, tile_start * TM_E, 0), axis=-1)
    pos = (group_row + rank).reshape(-1)
    src_rows = jnp.zeros((n_tiles * TM_E,), jnp.int32).at[pos].set(
        jnp.arange(t * TOP_K, dtype=jnp.int32) // TOP_K)

    used = tiles_e > 0
    used_before = jnp.cumsum(used.astype(jnp.int32)) - used.astype(jnp.int32)
    next_used = jnp.min(jnp.where((ar[None, :] > ar[:, None]) & used[None, :], ar[None, :], n_experts), axis=1)

    def items(nb):
        per_e = tiles_e * nb
        item_end = jnp.cumsum(per_e)
        item_start = item_end - per_e
        total = item_end[-1]
        s = jnp.arange(n_tiles * nb, dtype=jnp.int32)
        valid = s < total
        sc = jnp.minimum(s, total - 1)
        e = jnp.minimum(jnp.sum((item_end[None, :] <= sc[:, None]).astype(jnp.int32), axis=1), n_experts - 1)
        local = sc - item_start[e]
        ce = jnp.maximum(tiles_e[e], 1)
        n_in = local // ce
        m_in = tile_start[e] + local % ce
        spare = jnp.maximum(s - total, 0)
        m_out = jnp.where(valid, m_in, total_tiles + spare // nb)
        n_out = jnp.where(valid, n_in, spare % nb)
        first = valid & (local % ce == 0)
        slot = (used_before[e] * nb + n_in) % 2
        same_e = n_in + 1 < nb
        has_next = same_e | (next_used[e] < n_experts)
        next_e = jnp.where(same_e, e, jnp.minimum(next_used[e], n_experts - 1))
        next_n = jnp.where(same_e, n_in + 1, 0)
        fields = [None] * N_FIELDS
        fields[F_E], fields[F_N], fields[F_M], fields[F_OM], fields[F_ON] = e, n_in, m_in, m_out, n_out
        fields[F_VALID], fields[F_FIRST], fields[F_SLOT] = valid, first, slot
        fields[F_HAS_NEXT], fields[F_NEXT_E], fields[F_NEXT_N] = has_next, next_e, next_n
        return jnp.concatenate([f.astype(jnp.int32) for f in fields])

    return pos.astype(jnp.int32), src_rows, total_tiles.reshape(1).astype(jnp.int32), items(nb_in), items(nb_out)


def _moe(x, g, mods, rb, n_prompt_rows, seq_len, layer, w_router, b_router, w_gate, b_gate, w_up, b_up, w_down, b_down):
    t, d = x.shape
    n_experts = w_router.shape[1]
    n_tiles = t * TOP_K // TM_E + n_experts
    tn_in, tn_out = 512, 1024
    h, idx, probs, rank, counts = _normmod_router(x, g, mods, 3, 4, rb, w_router, b_router)
    pos, src_rows, total_tiles, tab_in, tab_out = _routing_tables(
        idx[:, :TOP_K], rank[:, :TOP_K], counts[0, :n_experts], n_tiles, w_gate.shape[3] // tn_in, d // tn_out)
    xs = _gather_rows(h, src_rows, total_tiles, n_tiles)
    a = _grouped_matmul(xs, [w_gate, w_up], [b_gate, b_up], layer, tab_in, n_tiles, tn_in, _swiglu, BF16, "expert_in")
    y = _grouped_matmul(a, [w_down], [b_down], layer, tab_out, n_tiles, tn_out, lambda v: v, F32, "expert_out")
    return _combine(x, y, pos, probs, mods, 5, n_prompt_rows, seq_len)


def kernel(x_prompt, x_sample, c_prompt, c_sample, state_conv, norm_mix_g, norm_ffn_g, ada_w, ada_b, conv_w_in, conv_w, conv_w_out, sg_w_proj, sg_b_proj, sg_ln_g, sg_ln_b, sg_w_s, sg_b_s, sg_w_out, router_w, router_b, exp_w_gate, exp_b_gate, exp_w_up, exp_b_up, exp_w_down, exp_b_down, final_norm_g):
    bp, sp, d = x_prompt.shape
    bs, ss, _ = x_sample.shape
    depth = norm_mix_g.shape[0]
    n_prompt_rows = bp * sp
    n_sample_rows = bs * ss
    assert bs == SLAB and n_sample_rows == TM and sp % TM == 0 and sg_w_s.shape[2] == SLAB

    x = jnp.concatenate([x_prompt.reshape(n_prompt_rows, d),
                         x_sample.transpose(1, 0, 2).reshape(n_sample_rows, d)], axis=0)

    n_c = bp + bs
    c_all = jnp.pad(jnp.concatenate([c_prompt, c_sample], axis=0), ((0, -n_c % 8), (0, 0)))
    mods_c = _ada(c_all, ada_w, ada_b)
    mods_all = jnp.concatenate(
        [jnp.repeat(mods_c[:, :bp], SLAB, axis=1), mods_c[:, bp:n_c]], axis=1)

    rb_nm = _mod_rowblock(TM_NM, n_prompt_rows, sp)
    rb_mm = _mod_rowblock(TM, n_prompt_rows, sp)

    conv_new_p, conv_new_s, chunk_new_s = [], [], []
    for i in range(depth):
        mods = mods_all[i]
        j = i // 2
        h = _normmod(x, norm_mix_g[i], mods, 0, 1, rb_nm, BF16)
        if i % 2 == 0:
            state = state_conv[j].transpose(1, 0, 2)
            u, cz = _conv_in(h, conv_w_in[j].astype(BF16), conv_w[j], state, n_prompt_rows, sp)
            x = _mm_res(u, conv_w_out[j].astype(BF16), x, mods, 2, rb_mm)
            conv_new_p.append(cz[:n_prompt_rows].reshape(bp, sp, d)[:, sp - 2:])
            conv_new_s.append(cz[n_prompt_rows:].reshape(ss, bs, d)[ss - 2:].transpose(1, 0, 2))
        else:
            z = _sg_proj(h, sg_w_proj[j].astype(BF16), sg_b_proj[j])
            vn = _layernorm_v(z, sg_ln_g[j], sg_ln_b[j])
            gp = _gate_prompt(z, vn, sg_w_s[j], sg_b_s[j].T, n_prompt_rows)
            ws_small = sg_w_s[j][:, :ss, :ss].reshape(-1, ss * ss)
            gs = _gate_sample(z, vn, ws_small, sg_b_s[j][:, :ss], n_prompt_rows, ss)
            gated = jnp.concatenate([gp, gs], axis=0)
            x = _mm_res(gated, sg_w_out[j].astype(BF16), x, mods, 2, rb_mm)
            chunk_new_s.append(vn[n_prompt_rows:].reshape(ss, bs, d).transpose(1, 0, 2))
        x = _moe(x, norm_ffn_g[i], mods, rb_nm, n_prompt_rows, sp, i, router_w[i], router_b[i],
                 exp_w_gate, exp_b_gate, exp_w_up, exp_b_up, exp_w_down, exp_b_down)

    y = _final_norm(x, final_norm_g)
    y_prompt = y[:n_prompt_rows].reshape(bp, sp, d)
    y_sample = y[n_prompt_rows:].reshape(ss, bs, d).transpose(1, 0, 2)
    return (y_prompt, y_sample, jnp.stack(conv_new_p), jnp.stack(conv_new_s), jnp.stack(chunk_new_s))
```

```python
import functools

import jax
import jax.numpy as jnp
from jax import lax
from jax.experimental import pallas as pl
from jax.experimental.pallas import tpu as pltpu

F32 = jnp.float32
BF16 = jnp.bfloat16

EPS = 1e-6
TOP_K = 4
SWIGLU_LIMIT = 7.0
SWIGLU_ALPHA = 1.702
N_MOD = 6
LANES = 128
SLAB = 128
VMEM_LIMIT = 56 * 1024 * 1024

TM = 1024
TM_E = 256
TM_NM = 256
TM_C = 128


def _params(sem, vmem=VMEM_LIMIT):
    return pltpu.CompilerParams(dimension_semantics=sem, vmem_limit_bytes=vmem)


def _dot(a, b):
    return jnp.dot(a, b, preferred_element_type=F32)


def _slab_bcast(y, row_fn):
    tm, n = y.shape
    return row_fn(y.reshape(tm // SLAB, SLAB, n)).reshape(tm, n)


def _ada_kernel(c_ref, w_ref, b_ref, o_ref):
    c = c_ref[...]
    s = (c * jax.nn.sigmoid(c)).astype(BF16)
    o_ref[0] = _dot(s, w_ref[0].astype(BF16)) + b_ref[0]


def _ada(c_all, ada_w, ada_b):
    depth, d, n6 = ada_w.shape
    rows = c_all.shape[0]
    tn = 512
    return pl.pallas_call(
        _ada_kernel,
        grid=(depth, n6 // tn),
        in_specs=[
            pl.BlockSpec((rows, d), lambda l, n: (0, 0)),
            pl.BlockSpec((1, d, tn), lambda l, n: (l, 0, n)),
            pl.BlockSpec((1, 1, tn), lambda l, n: (l, 0, n)),
        ],
        out_specs=pl.BlockSpec((1, rows, tn), lambda l, n: (l, 0, n)),
        out_shape=jax.ShapeDtypeStruct((depth, rows, n6), F32),
        compiler_params=_params(("arbitrary", "arbitrary")),
        name="ada",
    )(c_all, ada_w, ada_b.reshape(depth, 1, n6))


def _rms(x, g):
    return x * lax.rsqrt(jnp.mean(x * x, axis=-1, keepdims=True) + EPS) * g


def _normmod_kernel(x_ref, g_ref, sh_ref, sc_ref, o_ref):
    y = _rms(x_ref[...], g_ref[...])
    h = _slab_bcast(y, lambda y3: y3 * (1.0 + sc_ref[...])[None] + sh_ref[...][None])
    o_ref[...] = h.astype(o_ref.dtype)


def _mod_rowblock(tm, n_prompt_rows, seq_len):
    n_prompt_tiles = n_prompt_rows // tm
    tiles_per_seq = seq_len // tm
    n_seq = n_prompt_rows // seq_len
    return lambda m: jnp.where(m < n_prompt_tiles, m // tiles_per_seq, n_seq)


def _normmod(x, g, mods, k_shift, k_scale, rb, out_dtype):
    t, d = x.shape
    tm = TM_NM
    return pl.pallas_call(
        _normmod_kernel,
        grid=(t // tm,),
        in_specs=[
            pl.BlockSpec((tm, d), lambda m: (m, 0)),
            pl.BlockSpec((1, d), lambda m: (0, 0)),
            pl.BlockSpec((SLAB, d), lambda m: (rb(m), k_shift)),
            pl.BlockSpec((SLAB, d), lambda m: (rb(m), k_scale)),
        ],
        out_specs=pl.BlockSpec((tm, d), lambda m: (m, 0)),
        out_shape=jax.ShapeDtypeStruct((t, d), out_dtype),
        compiler_params=_params(("arbitrary",)),
        name="normmod",
    )(x, g.reshape(1, d), mods, mods)


def _normmod_router_kernel(n_experts, x_ref, g_ref, sh_ref, sc_ref, wr_ref, br_ref,
                           h_ref, idx_ref, p_ref, rank_ref, cnt_ref, base_ref):
    @pl.when(pl.program_id(0) == 0)
    def _():
        base_ref[...] = jnp.zeros(base_ref.shape, F32)

    y = _rms(x_ref[...], g_ref[...])
    h = _slab_bcast(y, lambda y3: y3 * (1.0 + sc_ref[...])[None] + sh_ref[...][None])
    h_ref[...] = h
    logits = jnp.dot(h, wr_ref[...], preferred_element_type=F32, precision=lax.Precision.HIGHEST) + br_ref[...]
    lane = lax.broadcasted_iota(jnp.int32, logits.shape, 1)
    neg = jnp.float32(-jnp.inf)
    l = jnp.where(lane < n_experts, logits, neg)
    vals, idxs = [], []
    for _ in range(TOP_K):
        mx = jnp.max(l, axis=-1, keepdims=True)
        ix = jnp.min(jnp.where(l == mx, lane, LANES), axis=-1, keepdims=True)
        vals.append(mx)
        idxs.append(ix)
        l = jnp.where(lane == ix, neg, l)
    es = [jnp.exp(v - vals[0]) for v in vals]
    tot = es[0] + es[1] + es[2] + es[3]
    idx_out = jnp.zeros(logits.shape, jnp.int32)
    p_out = jnp.zeros(logits.shape, F32)
    for k in range(TOP_K):
        idx_out = jnp.where(lane == k, idxs[k], idx_out)
        p_out = jnp.where(lane == k, es[k] / tot, p_out)
    idx_ref[...] = idx_out
    p_ref[...] = p_out

    tm = logits.shape[0]
    chosen = jnp.zeros(logits.shape, F32)
    for k in range(TOP_K):
        chosen = chosen + (lane == idxs[k]).astype(F32)
    row = lax.broadcasted_iota(jnp.int32, (tm, tm), 0)
    col = lax.broadcasted_iota(jnp.int32, (tm, tm), 1)
    before = _dot((col < row).astype(BF16), chosen.astype(BF16)) + base_ref[...]
    rank_out = jnp.zeros(logits.shape, jnp.int32)
    for k in range(TOP_K):
        r = jnp.sum(jnp.where(lane == idxs[k], before, 0.0), axis=-1, keepdims=True)
        rank_out = jnp.where(lane == k, r.astype(jnp.int32), rank_out)
    rank_ref[...] = rank_out
    base = base_ref[...] + jnp.sum(chosen, axis=0, keepdims=True)
    base_ref[...] = base
    cnt_ref[...] = jnp.broadcast_to(base, cnt_ref.shape).astype(jnp.int32)


def _normmod_router(x, g, mods, k_shift, k_scale, rb, w_router, b_router):
    t, d = x.shape
    n_experts = w_router.shape[1]
    tm = TM_NM
    wr = jnp.pad(w_router, ((0, 0), (0, LANES - n_experts)))
    br = jnp.pad(b_router, (0, LANES - n_experts)).reshape(1, LANES)
    return pl.pallas_call(
        functools.partial(_normmod_router_kernel, n_experts),
        grid=(t // tm,),
        in_specs=[
            pl.BlockSpec((tm, d), lambda m: (m, 0)),
            pl.BlockSpec((1, d), lambda m: (0, 0)),
            pl.BlockSpec((SLAB, d), lambda m: (rb(m), k_shift)),
            pl.BlockSpec((SLAB, d), lambda m: (rb(m), k_scale)),
            pl.BlockSpec((d, LANES), lambda m: (0, 0)),
            pl.BlockSpec((1, LANES), lambda m: (0, 0)),
        ],
        out_specs=[
            pl.BlockSpec((tm, d), lambda m: (m, 0)),
            pl.BlockSpec((tm, LANES), lambda m: (m, 0)),
            pl.BlockSpec((tm, LANES), lambda m: (m, 0)),
            pl.BlockSpec((tm, LANES), lambda m: (m, 0)),
            pl.BlockSpec((8, LANES), lambda m: (0, 0)),
        ],
        out_shape=[
            jax.ShapeDtypeStruct((t, d), F32),
            jax.ShapeDtypeStruct((t, LANES), jnp.int32),
            jax.ShapeDtypeStruct((t, LANES), F32),
            jax.ShapeDtypeStruct((t, LANES), jnp.int32),
            jax.ShapeDtypeStruct((8, LANES), jnp.int32),
        ],
        scratch_shapes=[pltpu.VMEM((1, LANES), F32)],
        compiler_params=_params(("arbitrary",)),
        name="normmod_router",
    )(x, g.reshape(1, d), mods, mods, wr, br)


def _final_norm_kernel(x_ref, g_ref, o_ref):
    o_ref[...] = _rms(x_ref[...], g_ref[...])


def _final_norm(x, g):
    t, d = x.shape
    tm = TM_NM
    return pl.pallas_call(
        _final_norm_kernel,
        grid=(t // tm,),
        in_specs=[pl.BlockSpec((tm, d), lambda m: (m, 0)), pl.BlockSpec((1, d), lambda m: (0, 0))],
        out_specs=pl.BlockSpec((tm, d), lambda m: (m, 0)),
        out_shape=jax.ShapeDtypeStruct((t, d), F32),
        compiler_params=_params(("arbitrary",)),
        name="final_norm",
    )(x, g.reshape(1, d))


def _conv_in_kernel(n_prompt_tiles, tiles_per_seq, h_ref, wb_ref, wc_ref, wz_ref, cw_ref, st_ref,
                    u_ref, cz_ref, carry_ref):
    m = pl.program_id(0)
    n = pl.program_id(1)
    h = h_ref[...]
    b = _dot(h, wb_ref[...])
    cz = _dot(h, wc_ref[...]) * _dot(h, wz_ref[...])
    cz_ref[...] = cz
    tm, tn = cz.shape
    w0 = cw_ref[0:1, :]
    w1 = cw_ref[1:2, :]
    w2 = cw_ref[2:3, :]

    @pl.when(m < n_prompt_tiles)
    def _():
        @pl.when(m % tiles_per_seq == 0)
        def _():
            carry_ref[n] = jnp.zeros((8, tn), F32)

        prev = carry_ref[n]
        row = lax.broadcasted_iota(jnp.int32, (tm, tn), 0)
        p1 = jnp.where(row == 0, prev[7:8, :], pltpu.roll(cz, 1, 0))
        p2 = jnp.where(row == 0, prev[6:7, :], jnp.where(row == 1, prev[7:8, :], pltpu.roll(cz, 2, 0)))
        u_ref[...] = (b * (w0 * p2 + w1 * p1 + w2 * cz)).astype(u_ref.dtype)
        carry_ref[n] = cz[tm - 8:, :]

    @pl.when(m >= n_prompt_tiles)
    def _():
        st0 = st_ref[0]
        st1 = st_ref[1]
        p1 = jnp.concatenate([st1, cz[:tm - SLAB, :]], axis=0)
        p2 = jnp.concatenate([st0, st1, cz[:tm - 2 * SLAB, :]], axis=0)
        u_ref[...] = (b * (w0 * p2 + w1 * p1 + w2 * cz)).astype(u_ref.dtype)


def _conv_in(h, w_in, conv_w, state, n_prompt_rows, seq_len):
    t, d = h.shape
    tm, tn = TM, 256
    nb = d // tn
    kern = functools.partial(_conv_in_kernel, n_prompt_rows // tm, seq_len // tm)
    return pl.pallas_call(
        kern,
        grid=(t // tm, nb),
        in_specs=[
            pl.BlockSpec((tm, d), lambda m, n: (m, 0)),
            pl.BlockSpec((d, tn), lambda m, n: (0, n)),
            pl.BlockSpec((d, tn), lambda m, n: (0, n + nb)),
            pl.BlockSpec((d, tn), lambda m, n: (0, n + 2 * nb)),
            pl.BlockSpec((3, tn), lambda m, n: (0, n)),
            pl.BlockSpec((2, SLAB, tn), lambda m, n: (0, 0, n)),
        ],
        out_specs=[
            pl.BlockSpec((tm, tn), lambda m, n: (m, n)),
            pl.BlockSpec((tm, tn), lambda m, n: (m, n)),
        ],
        out_shape=[jax.ShapeDtypeStruct((t, d), BF16), jax.ShapeDtypeStruct((t, d), F32)],
        scratch_shapes=[pltpu.VMEM((nb, 8, tn), F32)],
        compiler_params=_params(("arbitrary", "arbitrary")),
        name="conv_in",
    )(h, w_in, w_in, w_in, conv_w, state)


def _mm_res_kernel(a_ref, w_ref, x_ref, gate_ref, o_ref):
    acc = _dot(a_ref[...], w_ref[...])
    o_ref[...] = x_ref[...] + _slab_bcast(acc, lambda a3: a3 * gate_ref[...][None])


def _mm_res(a, w, x, mods, k_gate, rb):
    t, d = x.shape
    kdim = a.shape[1]
    tm, tn = TM, 512
    nb = d // tn
    return pl.pallas_call(
        _mm_res_kernel,
        grid=(t // tm, nb),
        in_specs=[
            pl.BlockSpec((tm, kdim), lambda m, n: (m, 0)),
            pl.BlockSpec((kdim, tn), lambda m, n: (0, n)),
            pl.BlockSpec((tm, tn), lambda m, n: (m, n)),
            pl.BlockSpec((SLAB, tn), lambda m, n: (rb(m), k_gate * nb + n)),
        ],
        out_specs=pl.BlockSpec((tm, tn), lambda m, n: (m, n)),
        out_shape=jax.ShapeDtypeStruct((t, d), F32),
        compiler_params=_params(("arbitrary", "arbitrary")),
        name="mm_res",
    )(a, w, x, mods)


def _sg_proj_kernel(h_ref, w_ref, b_ref, o_ref):
    z = _dot(h_ref[...], w_ref[...]) + b_ref[...]
    o_ref[...] = 0.5 * z * (1.0 + lax.erf(z * (2.0 ** -0.5)))


def _sg_proj(h, w, b):
    t, d = h.shape
    n2 = w.shape[1]
    tm, tn = TM, 512
    return pl.pallas_call(
        _sg_proj_kernel,
        grid=(t // tm, n2 // tn),
        in_specs=[
            pl.BlockSpec((tm, d), lambda m, n: (m, 0)),
            pl.BlockSpec((d, tn), lambda m, n: (0, n)),
            pl.BlockSpec((1, tn), lambda m, n: (0, n)),
        ],
        out_specs=pl.BlockSpec((tm, tn), lambda m, n: (m, n)),
        out_shape=jax.ShapeDtypeStruct((t, n2), F32),
        compiler_params=_params(("arbitrary", "arbitrary")),
        name="sg_proj",
    )(h, w, b.reshape(1, n2))


def _layernorm_kernel(v_ref, g_ref, b_ref, o_ref):
    v = v_ref[...]
    vc = v - jnp.mean(v, axis=-1, keepdims=True)
    var = jnp.mean(vc * vc, axis=-1, keepdims=True)
    o_ref[...] = vc * lax.rsqrt(var + EPS) * g_ref[...] + b_ref[...]


def _layernorm_v(z, g, b):
    t, n2 = z.shape
    d = n2 // 2
    tm = TM_NM
    return pl.pallas_call(
        _layernorm_kernel,
        grid=(t // tm,),
        in_specs=[
            pl.BlockSpec((tm, d), lambda m: (m, 1)),
            pl.BlockSpec((1, d), lambda m: (0, 0)),
            pl.BlockSpec((1, d), lambda m: (0, 0)),
        ],
        out_specs=pl.BlockSpec((tm, d), lambda m: (m, 0)),
        out_shape=jax.ShapeDtypeStruct((t, d), F32),
        compiler_params=_params(("arbitrary",)),
        name="layernorm_v",
    )(z, g.reshape(1, d), b.reshape(1, d))


def _gate_prompt_kernel(groups, u_ref, v_ref, ws_ref, bs_ref, o_ref):
    chunk = ws_ref.shape[1]
    hd = u_ref.shape[1] // groups
    row = lax.broadcasted_iota(jnp.int32, (chunk, chunk), 0)
    col = lax.broadcasted_iota(jnp.int32, (chunk, chunk), 1)
    for g in range(groups):
        ws = jnp.where(col <= row, ws_ref[g], 0.0).astype(BF16)
        vg = v_ref[:, g * hd:(g + 1) * hd].astype(BF16)
        mixed = _dot(ws, vg) + bs_ref[:, g:g + 1]
        o_ref[:, g * hd:(g + 1) * hd] = (u_ref[:, g * hd:(g + 1) * hd] * mixed).astype(o_ref.dtype)


def _gate_prompt(z, vn, w_s, b_s_t, n_prompt_rows):
    groups, chunk, _ = w_s.shape
    d = vn.shape[1]
    return pl.pallas_call(
        functools.partial(_gate_prompt_kernel, groups),
        grid=(n_prompt_rows // chunk,),
        in_specs=[
            pl.BlockSpec((chunk, d), lambda c: (c, 0)),
            pl.BlockSpec((chunk, d), lambda c: (c, 0)),
            pl.BlockSpec((groups, chunk, chunk), lambda c: (0, 0, 0)),
            pl.BlockSpec((chunk, groups), lambda c: (0, 0)),
        ],
        out_specs=pl.BlockSpec((chunk, d), lambda c: (c, 0)),
        out_shape=jax.ShapeDtypeStruct((n_prompt_rows, d), BF16),
        compiler_params=_params(("arbitrary",)),
        name="gate_prompt",
    )(z, vn, w_s, b_s_t)


def _gate_sample_kernel(n_pos, ws_ref, bs_ref, u_ref, v_ref, o_ref):
    g = pl.program_id(0)
    for t in range(n_pos):
        mixed = None
        for s in range(t + 1):
            term = ws_ref[g, t * n_pos + s] * v_ref[s * SLAB:(s + 1) * SLAB, :]
            mixed = term if mixed is None else mixed + term
        mixed = mixed + bs_ref[g, t]
        o_ref[t * SLAB:(t + 1) * SLAB, :] = (u_ref[t * SLAB:(t + 1) * SLAB, :] * mixed).astype(o_ref.dtype)


def _gate_sample(z, vn, ws_small, bs_small, n_prompt_rows, n_pos):
    groups = ws_small.shape[0]
    d = vn.shape[1]
    hd = d // groups
    rows = n_pos * SLAB
    blk = n_prompt_rows // rows
    return pl.pallas_call(
        functools.partial(_gate_sample_kernel, n_pos),
        grid=(groups,),
        in_specs=[
            pl.BlockSpec(memory_space=pltpu.SMEM),
            pl.BlockSpec(memory_space=pltpu.SMEM),
            pl.BlockSpec((rows, hd), lambda g: (blk, g)),
            pl.BlockSpec((rows, hd), lambda g: (blk, g)),
        ],
        out_specs=pl.BlockSpec((rows, hd), lambda g: (0, g)),
        out_shape=jax.ShapeDtypeStruct((rows, d), BF16),
        compiler_params=_params(("arbitrary",)),
        name="gate_sample",
    )(ws_small, bs_small, z, vn)


def _gather_kernel(total_ref, src_ref, nxt_ref, h_hbm, o_ref, buf, sem):
    i = pl.program_id(0)
    tm = buf.shape[1]
    total = total_ref[0]
    slot = i % 2

    def start(rows_ref, s):
        def issue(j, carry):
            r = rows_ref[0, 0, j]
            pltpu.make_async_copy(h_hbm.at[pl.ds(r, 1), :], buf.at[s, pl.ds(j, 1), :], sem.at[s]).start()
            return carry

        lax.fori_loop(0, tm, issue, 0, unroll=8)

    @pl.when(i == 0)
    def _():
        start(src_ref, 0)

    @pl.when(i + 1 < total)
    def _():
        start(nxt_ref, 1 - slot)

    @pl.when(i < total)
    def _():
        pltpu.make_async_copy(h_hbm.at[pl.ds(0, tm), :], buf.at[slot], sem.at[slot]).wait()
        o_ref[...] = buf[slot].astype(o_ref.dtype)

    @pl.when(i >= total)
    def _():
        o_ref[...] = jnp.zeros(o_ref.shape, o_ref.dtype)


def _gather_rows(h, src_rows, total_tiles, n_tiles):
    t, d = h.shape
    tm = TM_E
    return pl.pallas_call(
        _gather_kernel,
        grid_spec=pltpu.PrefetchScalarGridSpec(
            num_scalar_prefetch=1,
            grid=(n_tiles,),
            in_specs=[
                pl.BlockSpec((1, 1, tm), lambda i, tot: (i, 0, 0), memory_space=pltpu.SMEM),
                pl.BlockSpec((1, 1, tm), lambda i, tot: (jnp.minimum(i + 1, n_tiles - 1), 0, 0),
                             memory_space=pltpu.SMEM),
                pl.BlockSpec(memory_space=pl.ANY),
            ],
            out_specs=pl.BlockSpec((tm, d), lambda i, tot: (i, 0)),
            scratch_shapes=[pltpu.VMEM((2, tm, d), F32), pltpu.SemaphoreType.DMA((2,))],
        ),
        out_shape=jax.ShapeDtypeStruct((n_tiles * tm, d), BF16),
        compiler_params=_params(("arbitrary",)),
        name="moe_gather",
    )(total_tiles, src_rows.reshape(n_tiles, 1, tm), src_rows.reshape(n_tiles, 1, tm), h)


(F_E, F_N, F_M, F_OM, F_ON, F_VALID, F_FIRST, F_SLOT, F_PF_LO, F_PF_HI, F_NEXT_E, F_NEXT_N, N_FIELDS) = range(13)
K_CHUNK = 512


def _grouped_matmul_kernel(layer, n_items, n_w, epilogue, tab, x_ref, *refs):
    w_hbm = refs[:n_w]
    b_refs = refs[n_w:2 * n_w]
    o_ref = refs[2 * n_w]
    w_buf = refs[2 * n_w + 1:3 * n_w + 1]
    w_bf = refs[3 * n_w + 1:4 * n_w + 1]
    sem = refs[4 * n_w + 1]
    s = pl.program_id(0)
    fld = lambda f: tab[f * n_items + s]
    kdim, tn = w_bf[0].shape
    valid = fld(F_VALID)
    first = fld(F_FIRST)
    slot = fld(F_SLOT)

    n_chunks = kdim // K_CHUNK

    def copies(e, n, sl, rows):
        col = pl.multiple_of(n * tn, tn)
        return [pltpu.make_async_copy(w_hbm[i].at[layer, e, rows, pl.ds(col, tn)], w_buf[i].at[sl, rows, :],
                                      sem.at[i, sl]) for i in range(n_w)]

    @pl.when(s == 0)
    def _():
        for c in copies(fld(F_E), fld(F_N), 0, pl.ds(0, kdim)):
            c.start()

    pf_lo = fld(F_PF_LO)
    pf_hi = fld(F_PF_HI)

    def prefetch_next():
        for c in range(n_chunks):
            @pl.when((pf_lo <= c) & (c < pf_hi))
            def _():
                for cp in copies(fld(F_NEXT_E), fld(F_NEXT_N), 1 - slot, pl.ds(c * K_CHUNK, K_CHUNK)):
                    cp.start()

    @pl.when((valid == 1) & (first == 1))
    def _():
        for c in copies(0, 0, slot, pl.ds(0, kdim)):
            c.wait()
        prefetch_next()
        acc = [None] * n_w
        for c in range(kdim // K_CHUNK):
            ks = slice(c * K_CHUNK, (c + 1) * K_CHUNK)
            xk = x_ref[:, ks]
            for i in range(n_w):
                w = w_buf[i][slot, ks, :].astype(BF16)
                w_bf[i][ks, :] = w
                p = _dot(xk, w)
                acc[i] = p if acc[i] is None else acc[i] + p
        o_ref[...] = epilogue(*[acc[i] + b_refs[i][0] for i in range(n_w)]).astype(o_ref.dtype)

    @pl.when((valid == 1) & (first == 0))
    def _():
        prefetch_next()
        x = x_ref[...]
        o_ref[...] = epilogue(*[_dot(x, w_bf[i][...]) + b_refs[i][0] for i in range(n_w)]).astype(o_ref.dtype)

    @pl.when(valid == 0)
    def _():
        o_ref[...] = jnp.zeros(o_ref.shape, o_ref.dtype)


def _swiglu(g, u):
    g = jnp.minimum(g, SWIGLU_LIMIT)
    u = jnp.clip(u, -SWIGLU_LIMIT, SWIGLU_LIMIT)
    return (u + 1.0) * (g * jax.nn.sigmoid(SWIGLU_ALPHA * g))


def _grouped_matmul(x, weights, biases, layer, tab, n_tiles, tn, epilogue, out_dtype, name):
    depth, n_e, kdim, n_out = weights[0].shape
    n_w = len(weights)
    tm = TM_E
    n_items = n_tiles * (n_out // tn)
    bias_spec = pl.BlockSpec((1, 1, tn), lambda s, t: (layer * n_e + t[F_E * n_items + s], 0, t[F_N * n_items + s]))
    return pl.pallas_call(
        functools.partial(_grouped_matmul_kernel, layer, n_items, n_w, epilogue),
        grid_spec=pltpu.PrefetchScalarGridSpec(
            num_scalar_prefetch=1,
            grid=(n_items,),
            in_specs=[pl.BlockSpec((tm, kdim), lambda s, t: (t[F_M * n_items + s], 0))]
            + [pl.BlockSpec(memory_space=pl.ANY)] * n_w
            + [bias_spec] * n_w,
            out_specs=pl.BlockSpec((tm, tn), lambda s, t: (t[F_OM * n_items + s], t[F_ON * n_items + s])),
            scratch_shapes=[pltpu.VMEM((2, kdim, tn), F32)] * n_w
            + [pltpu.VMEM((kdim, tn), BF16)] * n_w
            + [pltpu.SemaphoreType.DMA((n_w, 2))],
        ),
        out_shape=jax.ShapeDtypeStruct((n_tiles * tm, n_out), out_dtype),
        compiler_params=_params(("arbitrary",)),
        name=name,
    )(tab, x, *weights, *[b.reshape(depth * n_e, 1, n_out) for b in biases])


def _combine_kernel(pos_ref, nxt_ref, x_ref, gate_ref, p_ref, y_hbm, o_ref, buf, sem):
    i = pl.program_id(0)
    tm = x_ref.shape[0]
    slot = i % 2

    def start(rows_ref, s):
        def issue(j, carry):
            for k in range(TOP_K):
                r = rows_ref[0, 0, j * TOP_K + k]
                pltpu.make_async_copy(y_hbm.at[pl.ds(r, 1), :], buf.at[s, k, pl.ds(j, 1), :], sem.at[s]).start()
            return carry

        lax.fori_loop(0, tm, issue, 0, unroll=4)

    @pl.when(i == 0)
    def _():
        start(pos_ref, 0)

    @pl.when(i + 1 < pl.num_programs(0))
    def _():
        start(nxt_ref, 1 - slot)

    for k in range(TOP_K):
        pltpu.make_async_copy(y_hbm.at[pl.ds(0, tm), :], buf.at[slot, k], sem.at[slot]).wait()
    f = p_ref[:, 0:1] * buf[slot, 0]
    for k in range(1, TOP_K):
        f = f + p_ref[:, k:k + 1] * buf[slot, k]
    o_ref[...] = x_ref[...] + gate_ref[...] * f


def _combine(x, y, pos, probs, mods, k_gate, n_prompt_rows, seq_len):
    t, d = x.shape
    tm = TM_C
    rb = _mod_rowblock(tm, n_prompt_rows, seq_len)
    return pl.pallas_call(
        _combine_kernel,
        grid=(t // tm,),
        in_specs=[
            pl.BlockSpec((1, 1, tm * TOP_K), lambda m: (m, 0, 0), memory_space=pltpu.SMEM),
            pl.BlockSpec((1, 1, tm * TOP_K), lambda m: (jnp.minimum(m + 1, t // tm - 1), 0, 0),
                         memory_space=pltpu.SMEM),
            pl.BlockSpec((tm, d), lambda m: (m, 0)),
            pl.BlockSpec((SLAB, d), lambda m: (rb(m), k_gate)),
            pl.BlockSpec((tm, LANES), lambda m: (m, 0)),
            pl.BlockSpec(memory_space=pl.ANY),
        ],
        out_specs=pl.BlockSpec((tm, d), lambda m: (m, 0)),
        out_shape=jax.ShapeDtypeStruct((t, d), F32),
        scratch_shapes=[pltpu.VMEM((2, TOP_K, tm, d), F32), pltpu.SemaphoreType.DMA((2,))],
        compiler_params=_params(("arbitrary",)),
        name="moe_combine",
    )(pos.reshape(t // tm, 1, tm * TOP_K), pos.reshape(t // tm, 1, tm * TOP_K), x, mods, probs, y)


def _routing_tables(top_idx, rank, counts, n_tiles, n_chunks, nb_in, nb_out):
    t = top_idx.shape[0]
    n_experts = counts.shape[0]
    ar = jnp.arange(n_experts, dtype=jnp.int32)
    tiles_e = (counts + TM_E - 1) // TM_E
    tile_end = jnp.cumsum(tiles_e)
    tile_start = tile_end - tiles_e
    total_tiles = tile_end[-1]
    group_row = jnp.sum(jnp.where(top_idx[:, :, None] == ar, tile_start * TM_E, 0), axis=-1)
    pos = (group_row + rank).reshape(-1)
    src_rows = jnp.zeros((n_tiles * TM_E,), jnp.int32).at[pos].set(
        jnp.arange(t * TOP_K, dtype=jnp.int32) // TOP_K)

    used = tiles_e > 0
    used_before = jnp.cumsum(used.astype(jnp.int32)) - used.astype(jnp.int32)
    next_used = jnp.min(jnp.where((ar[None, :] > ar[:, None]) & used[None, :], ar[None, :], n_experts), axis=1)

    def items(nb):
        per_e = tiles_e * nb
        item_end = jnp.cumsum(per_e)
        item_start = item_end - per_e
        total = item_end[-1]
        s = jnp.arange(n_tiles * nb, dtype=jnp.int32)
        valid = s < total
        sc = jnp.minimum(s, total - 1)
        e = jnp.minimum(jnp.sum((item_end[None, :] <= sc[:, None]).astype(jnp.int32), axis=1), n_experts - 1)
        local = sc - item_start[e]
        ce = jnp.maximum(tiles_e[e], 1)
        n_in = local // ce
        m_in = tile_start[e] + local % ce
        spare = jnp.maximum(s - total, 0)
        m_out = jnp.where(valid, m_in, total_tiles + spare // nb)
        n_out = jnp.where(valid, n_in, spare % nb)
        first = valid & (local % ce == 0)
        slot = (used_before[e] * nb + n_in) % 2
        same_e = n_in + 1 < nb
        has_next = valid & (same_e | (next_used[e] < n_experts))
        next_e = jnp.where(same_e, e, jnp.minimum(next_used[e], n_experts - 1))
        next_n = jnp.where(same_e, n_in + 1, 0)
        j = local % ce
        pf_lo = jnp.where(has_next, j * n_chunks // ce, 0)
        pf_hi = jnp.where(has_next, (j + 1) * n_chunks // ce, 0)
        fields = [None] * N_FIELDS
        fields[F_E], fields[F_N], fields[F_M], fields[F_OM], fields[F_ON] = e, n_in, m_in, m_out, n_out
        fields[F_VALID], fields[F_FIRST], fields[F_SLOT] = valid, first, slot
        fields[F_PF_LO], fields[F_PF_HI], fields[F_NEXT_E], fields[F_NEXT_N] = pf_lo, pf_hi, next_e, next_n
        return jnp.concatenate([f.astype(jnp.int32) for f in fields])

    return pos.astype(jnp.int32), src_rows, total_tiles.reshape(1).astype(jnp.int32), items(nb_in), items(nb_out)


def _moe(x, g, mods, rb, n_prompt_rows, seq_len, layer, w_router, b_router, w_gate, b_gate, w_up, b_up, w_down, b_down):
    t, d = x.shape
    n_experts = w_router.shape[1]
    n_tiles = t * TOP_K // TM_E + n_experts
    tn_in, tn_out = 512, 1024
    assert w_gate.shape[2] == d and w_down.shape[2] == d
    h, idx, probs, rank, counts = _normmod_router(x, g, mods, 3, 4, rb, w_router, b_router)
    pos, src_rows, total_tiles, tab_in, tab_out = _routing_tables(
        idx[:, :TOP_K], rank[:, :TOP_K], counts[0, :n_experts], n_tiles, d // K_CHUNK,
        w_gate.shape[3] // tn_in, d // tn_out)
    xs = _gather_rows(h, src_rows, total_tiles, n_tiles)
    a = _grouped_matmul(xs, [w_gate, w_up], [b_gate, b_up], layer, tab_in, n_tiles, tn_in, _swiglu, BF16, "expert_in")
    y = _grouped_matmul(a, [w_down], [b_down], layer, tab_out, n_tiles, tn_out, lambda v: v, F32, "expert_out")
    return _combine(x, y, pos, probs, mods, 5, n_prompt_rows, seq_len)


def kernel(x_prompt, x_sample, c_prompt, c_sample, state_conv, norm_mix_g, norm_ffn_g, ada_w, ada_b, conv_w_in, conv_w, conv_w_out, sg_w_proj, sg_b_proj, sg_ln_g, sg_ln_b, sg_w_s, sg_b_s, sg_w_out, router_w, router_b, exp_w_gate, exp_b_gate, exp_w_up, exp_b_up, exp_w_down, exp_b_down, final_norm_g):
    bp, sp, d = x_prompt.shape
    bs, ss, _ = x_sample.shape
    depth = norm_mix_g.shape[0]
    n_prompt_rows = bp * sp
    n_sample_rows = bs * ss
    assert bs == SLAB and n_sample_rows == TM and sp % TM == 0 and sg_w_s.shape[2] == SLAB

    x = jnp.concatenate([x_prompt.reshape(n_prompt_rows, d),
                         x_sample.transpose(1, 0, 2).reshape(n_sample_rows, d)], axis=0)

    n_c = bp + bs
    c_all = jnp.pad(jnp.concatenate([c_prompt, c_sample], axis=0), ((0, -n_c % 8), (0, 0)))
    mods_c = _ada(c_all, ada_w, ada_b)
    mods_all = jnp.concatenate(
        [jnp.repeat(mods_c[:, :bp], SLAB, axis=1), mods_c[:, bp:n_c]], axis=1)

    rb_nm = _mod_rowblock(TM_NM, n_prompt_rows, sp)
    rb_mm = _mod_rowblock(TM, n_prompt_rows, sp)

    conv_new_p, conv_new_s, chunk_new_s = [], [], []
    for i in range(depth):
        mods = mods_all[i]
        j = i // 2
        h = _normmod(x, norm_mix_g[i], mods, 0, 1, rb_nm, BF16)
        if i % 2 == 0:
            state = state_conv[j].transpose(1, 0, 2)
            u, cz = _conv_in(h, conv_w_in[j].astype(BF16), conv_w[j], state, n_prompt_rows, sp)
            x = _mm_res(u, conv_w_out[j].astype(BF16), x, mods, 2, rb_mm)
            conv_new_p.append(cz[:n_prompt_rows].reshape(bp, sp, d)[:, sp - 2:])
            conv_new_s.append(cz[n_prompt_rows:].reshape(ss, bs, d)[ss - 2:].transpose(1, 0, 2))
        else:
            z = _sg_proj(h, sg_w_proj[j].astype(BF16), sg_b_proj[j])
            vn = _layernorm_v(z, sg_ln_g[j], sg_ln_b[j])
            gp = _gate_prompt(z, vn, sg_w_s[j], sg_b_s[j].T, n_prompt_rows)
            ws_small = sg_w_s[j][:, :ss, :ss].reshape(-1, ss * ss)
            gs = _gate_sample(z, vn, ws_small, sg_b_s[j][:, :ss], n_prompt_rows, ss)
            gated = jnp.concatenate([gp, gs], axis=0)
            x = _mm_res(gated, sg_w_out[j].astype(BF16), x, mods, 2, rb_mm)
            chunk_new_s.append(vn[n_prompt_rows:].reshape(ss, bs, d).transpose(1, 0, 2))
        x = _moe(x, norm_ffn_g[i], mods, rb_nm, n_prompt_rows, sp, i, router_w[i], router_b[i],
                 exp_w_gate, exp_b_gate, exp_w_up, exp_b_up, exp_w_down, exp_b_down)

    y = _final_norm(x, final_norm_g)
    y_prompt = y[:n_prompt_rows].reshape(bp, sp, d)
    y_sample = y[n_prompt_rows:].reshape(ss, bs, d).transpose(1, 0, 2)
    return (y_prompt, y_sample, jnp.stack(conv_new_p), jnp.stack(conv_new_s), jnp.stack(chunk_new_s))
```

```python
import functools

import jax
import jax.numpy as jnp
from jax import lax
from jax.experimental import pallas as pl
from jax.experimental.pallas import tpu as pltpu

F32 = jnp.float32
BF16 = jnp.bfloat16

EPS = 1e-6
TOP_K = 4
SWIGLU_LIMIT = 7.0
SWIGLU_ALPHA = 1.702
N_MOD = 6
LANES = 128
SLAB = 128
VMEM_LIMIT = 56 * 1024 * 1024

TM = 1024
TM_E = 256
TM_NM = 256
TM_C = 128


def _params(sem, vmem=VMEM_LIMIT):
    return pltpu.CompilerParams(dimension_semantics=sem, vmem_limit_bytes=vmem)


def _dot(a, b):
    return jnp.dot(a, b, preferred_element_type=F32)


def _slab_bcast(y, row_fn):
    tm, n = y.shape
    return row_fn(y.reshape(tm // SLAB, SLAB, n)).reshape(tm, n)


def _ada_kernel(c_ref, w_ref, b_ref, o_ref):
    c = c_ref[...]
    s = (c * jax.nn.sigmoid(c)).astype(BF16)
    o_ref[0] = _dot(s, w_ref[0].astype(BF16)) + b_ref[0]


def _ada(c_all, ada_w, ada_b):
    depth, d, n6 = ada_w.shape
    rows = c_all.shape[0]
    tn = 512
    return pl.pallas_call(
        _ada_kernel,
        grid=(depth, n6 // tn),
        in_specs=[
            pl.BlockSpec((rows, d), lambda l, n: (0, 0)),
            pl.BlockSpec((1, d, tn), lambda l, n: (l, 0, n)),
            pl.BlockSpec((1, 1, tn), lambda l, n: (l, 0, n)),
        ],
        out_specs=pl.BlockSpec((1, rows, tn), lambda l, n: (l, 0, n)),
        out_shape=jax.ShapeDtypeStruct((depth, rows, n6), F32),
        compiler_params=_params(("arbitrary", "arbitrary")),
        name="ada",
    )(c_all, ada_w, ada_b.reshape(depth, 1, n6))


def _rms(x, g):
    return x * lax.rsqrt(jnp.mean(x * x, axis=-1, keepdims=True) + EPS) * g


def _normmod_kernel(x_ref, g_ref, sh_ref, sc_ref, o_ref):
    y = _rms(x_ref[...], g_ref[...])
    h = _slab_bcast(y, lambda y3: y3 * (1.0 + sc_ref[...])[None] + sh_ref[...][None])
    o_ref[...] = h.astype(o_ref.dtype)


def _mod_rowblock(tm, n_prompt_rows, seq_len):
    n_prompt_tiles = n_prompt_rows // tm
    tiles_per_seq = seq_len // tm
    n_seq = n_prompt_rows // seq_len
    return lambda m: jnp.where(m < n_prompt_tiles, m // tiles_per_seq, n_seq)


def _normmod(x, g, mods, k_shift, k_scale, rb, out_dtype):
    t, d = x.shape
    tm = TM_NM
    return pl.pallas_call(
        _normmod_kernel,
        grid=(t // tm,),
        in_specs=[
            pl.BlockSpec((tm, d), lambda m: (m, 0)),
            pl.BlockSpec((1, d), lambda m: (0, 0)),
            pl.BlockSpec((SLAB, d), lambda m: (rb(m), k_shift)),
            pl.BlockSpec((SLAB, d), lambda m: (rb(m), k_scale)),
        ],
        out_specs=pl.BlockSpec((tm, d), lambda m: (m, 0)),
        out_shape=jax.ShapeDtypeStruct((t, d), out_dtype),
        compiler_params=_params(("arbitrary",)),
        name="normmod",
    )(x, g.reshape(1, d), mods, mods)


def _normmod_router_kernel(n_experts, x_ref, g_ref, sh_ref, sc_ref, wr_ref, br_ref,
                           h_ref, idx_ref, p_ref, rank_ref, cnt_ref, base_ref):
    @pl.when(pl.program_id(0) == 0)
    def _():
        base_ref[...] = jnp.zeros(base_ref.shape, F32)

    y = _rms(x_ref[...], g_ref[...])
    h = _slab_bcast(y, lambda y3: y3 * (1.0 + sc_ref[...])[None] + sh_ref[...][None])
    h_ref[...] = h
    logits = jnp.dot(h, wr_ref[...], preferred_element_type=F32, precision=lax.Precision.HIGHEST) + br_ref[...]
    lane = lax.broadcasted_iota(jnp.int32, logits.shape, 1)
    neg = jnp.float32(-jnp.inf)
    l = jnp.where(lane < n_experts, logits, neg)
    vals, idxs = [], []
    for _ in range(TOP_K):
        mx = jnp.max(l, axis=-1, keepdims=True)
        ix = jnp.min(jnp.where(l == mx, lane, LANES), axis=-1, keepdims=True)
        vals.append(mx)
        idxs.append(ix)
        l = jnp.where(lane == ix, neg, l)
    es = [jnp.exp(v - vals[0]) for v in vals]
    tot = es[0] + es[1] + es[2] + es[3]
    idx_out = jnp.zeros(logits.shape, jnp.int32)
    p_out = jnp.zeros(logits.shape, F32)
    for k in range(TOP_K):
        idx_out = jnp.where(lane == k, idxs[k], idx_out)
        p_out = jnp.where(lane == k, es[k] / tot, p_out)
    idx_ref[...] = idx_out
    p_ref[...] = p_out

    tm = logits.shape[0]
    chosen = jnp.zeros(logits.shape, F32)
    for k in range(TOP_K):
        chosen = chosen + (lane == idxs[k]).astype(F32)
    row = lax.broadcasted_iota(jnp.int32, (tm, tm), 0)
    col = lax.broadcasted_iota(jnp.int32, (tm, tm), 1)
    before = _dot((col < row).astype(BF16), chosen.astype(BF16)) + base_ref[...]
    rank_out = jnp.zeros(logits.shape, jnp.int32)
    for k in range(TOP_K):
        r = jnp.sum(jnp.where(lane == idxs[k], before, 0.0), axis=-1, keepdims=True)
        rank_out = jnp.where(lane == k, r.astype(jnp.int32), rank_out)
    rank_ref[...] = rank_out
    base = base_ref[...] + jnp.sum(chosen, axis=0, keepdims=True)
    base_ref[...] = base
    cnt_ref[...] = jnp.broadcast_to(base, cnt_ref.shape).astype(jnp.int32)


def _normmod_router(x, g, mods, k_shift, k_scale, rb, w_router, b_router):
    t, d = x.shape
    n_experts = w_router.shape[1]
    tm = TM_NM
    wr = jnp.pad(w_router, ((0, 0), (0, LANES - n_experts)))
    br = jnp.pad(b_router, (0, LANES - n_experts)).reshape(1, LANES)
    return pl.pallas_call(
        functools.partial(_normmod_router_kernel, n_experts),
        grid=(t // tm,),
        in_specs=[
            pl.BlockSpec((tm, d), lambda m: (m, 0)),
            pl.BlockSpec((1, d), lambda m: (0, 0)),
            pl.BlockSpec((SLAB, d), lambda m: (rb(m), k_shift)),
            pl.BlockSpec((SLAB, d), lambda m: (rb(m), k_scale)),
            pl.BlockSpec((d, LANES), lambda m: (0, 0)),
            pl.BlockSpec((1, LANES), lambda m: (0, 0)),
        ],
        out_specs=[
            pl.BlockSpec((tm, d), lambda m: (m, 0)),
            pl.BlockSpec((tm, LANES), lambda m: (m, 0)),
            pl.BlockSpec((tm, LANES), lambda m: (m, 0)),
            pl.BlockSpec((tm, LANES), lambda m: (m, 0)),
            pl.BlockSpec((8, LANES), lambda m: (0, 0)),
        ],
        out_shape=[
            jax.ShapeDtypeStruct((t, d), F32),
            jax.ShapeDtypeStruct((t, LANES), jnp.int32),
            jax.ShapeDtypeStruct((t, LANES), F32),
            jax.ShapeDtypeStruct((t, LANES), jnp.int32),
            jax.ShapeDtypeStruct((8, LANES), jnp.int32),
        ],
        scratch_shapes=[pltpu.VMEM((1, LANES), F32)],
        compiler_params=_params(("arbitrary",)),
        name="normmod_router",
    )(x, g.reshape(1, d), mods, mods, wr, br)


def _final_norm_kernel(x_ref, g_ref, o_ref):
    o_ref[...] = _rms(x_ref[...], g_ref[...])


def _final_norm(x, g):
    t, d = x.shape
    tm = TM_NM
    return pl.pallas_call(
        _final_norm_kernel,
        grid=(t // tm,),
        in_specs=[pl.BlockSpec((tm, d), lambda m: (m, 0)), pl.BlockSpec((1, d), lambda m: (0, 0))],
        out_specs=pl.BlockSpec((tm, d), lambda m: (m, 0)),
        out_shape=jax.ShapeDtypeStruct((t, d), F32),
        compiler_params=_params(("arbitrary",)),
        name="final_norm",
    )(x, g.reshape(1, d))


def _conv_in_kernel(n_prompt_tiles, tiles_per_seq, h_ref, wb_ref, wc_ref, wz_ref, cw_ref, st_ref,
                    u_ref, cz_ref, carry_ref):
    m = pl.program_id(0)
    n = pl.program_id(1)
    h = h_ref[...]
    b = _dot(h, wb_ref[...])
    cz = _dot(h, wc_ref[...]) * _dot(h, wz_ref[...])
    cz_ref[...] = cz
    tm, tn = cz.shape
    w0 = cw_ref[0:1, :]
    w1 = cw_ref[1:2, :]
    w2 = cw_ref[2:3, :]

    @pl.when(m < n_prompt_tiles)
    def _():
        @pl.when(m % tiles_per_seq == 0)
        def _():
            carry_ref[n] = jnp.zeros((8, tn), F32)

        prev = carry_ref[n]
        row = lax.broadcasted_iota(jnp.int32, (tm, tn), 0)
        p1 = jnp.where(row == 0, prev[7:8, :], pltpu.roll(cz, 1, 0))
        p2 = jnp.where(row == 0, prev[6:7, :], jnp.where(row == 1, prev[7:8, :], pltpu.roll(cz, 2, 0)))
        u_ref[...] = (b * (w0 * p2 + w1 * p1 + w2 * cz)).astype(u_ref.dtype)
        carry_ref[n] = cz[tm - 8:, :]

    @pl.when(m >= n_prompt_tiles)
    def _():
        st0 = st_ref[0]
        st1 = st_ref[1]
        p1 = jnp.concatenate([st1, cz[:tm - SLAB, :]], axis=0)
        p2 = jnp.concatenate([st0, st1, cz[:tm - 2 * SLAB, :]], axis=0)
        u_ref[...] = (b * (w0 * p2 + w1 * p1 + w2 * cz)).astype(u_ref.dtype)


def _conv_in(h, w_in, conv_w, state, n_prompt_rows, seq_len):
    t, d = h.shape
    tm, tn = TM, 256
    nb = d // tn
    kern = functools.partial(_conv_in_kernel, n_prompt_rows // tm, seq_len // tm)
    return pl.pallas_call(
        kern,
        grid=(t // tm, nb),
        in_specs=[
            pl.BlockSpec((tm, d), lambda m, n: (m, 0)),
            pl.BlockSpec((d, tn), lambda m, n: (0, n)),
            pl.BlockSpec((d, tn), lambda m, n: (0, n + nb)),
            pl.BlockSpec((d, tn), lambda m, n: (0, n + 2 * nb)),
            pl.BlockSpec((3, tn), lambda m, n: (0, n)),
            pl.BlockSpec((2, SLAB, tn), lambda m, n: (0, 0, n)),
        ],
        out_specs=[
            pl.BlockSpec((tm, tn), lambda m, n: (m, n)),
            pl.BlockSpec((tm, tn), lambda m, n: (m, n)),
        ],
        out_shape=[jax.ShapeDtypeStruct((t, d), BF16), jax.ShapeDtypeStruct((t, d), F32)],
        scratch_shapes=[pltpu.VMEM((nb, 8, tn), F32)],
        compiler_params=_params(("arbitrary", "arbitrary")),
        name="conv_in",
    )(h, w_in, w_in, w_in, conv_w, state)


def _mm_res_kernel(a_ref, w_ref, x_ref, gate_ref, o_ref):
    acc = _dot(a_ref[...], w_ref[...])
    o_ref[...] = x_ref[...] + _slab_bcast(acc, lambda a3: a3 * gate_ref[...][None])


def _mm_res(a, w, x, mods, k_gate, rb):
    t, d = x.shape
    kdim = a.shape[1]
    tm, tn = TM, 512
    nb = d // tn
    return pl.pallas_call(
        _mm_res_kernel,
        grid=(t // tm, nb),
        in_specs=[
            pl.BlockSpec((tm, kdim), lambda m, n: (m, 0)),
            pl.BlockSpec((kdim, tn), lambda m, n: (0, n)),
            pl.BlockSpec((tm, tn), lambda m, n: (m, n)),
            pl.BlockSpec((SLAB, tn), lambda m, n: (rb(m), k_gate * nb + n)),
        ],
        out_specs=pl.BlockSpec((tm, tn), lambda m, n: (m, n)),
        out_shape=jax.ShapeDtypeStruct((t, d), F32),
        compiler_params=_params(("arbitrary", "arbitrary")),
        name="mm_res",
    )(a, w, x, mods)


def _sg_proj_kernel(h_ref, w_ref, b_ref, o_ref):
    z = _dot(h_ref[...], w_ref[...]) + b_ref[...]
    o_ref[...] = 0.5 * z * (1.0 + lax.erf(z * (2.0 ** -0.5)))


def _sg_proj(h, w, b):
    t, d = h.shape
    n2 = w.shape[1]
    tm, tn = TM, 512
    return pl.pallas_call(
        _sg_proj_kernel,
        grid=(t // tm, n2 // tn),
        in_specs=[
            pl.BlockSpec((tm, d), lambda m, n: (m, 0)),
            pl.BlockSpec((d, tn), lambda m, n: (0, n)),
            pl.BlockSpec((1, tn), lambda m, n: (0, n)),
        ],
        out_specs=pl.BlockSpec((tm, tn), lambda m, n: (m, n)),
        out_shape=jax.ShapeDtypeStruct((t, n2), F32),
        compiler_params=_params(("arbitrary", "arbitrary")),
        name="sg_proj",
    )(h, w, b.reshape(1, n2))


def _layernorm_kernel(v_ref, g_ref, b_ref, o_ref):
    v = v_ref[...]
    vc = v - jnp.mean(v, axis=-1, keepdims=True)
    var = jnp.mean(vc * vc, axis=-1, keepdims=True)
    o_ref[...] = vc * lax.rsqrt(var + EPS) * g_ref[...] + b_ref[...]


def _layernorm_v(z, g, b):
    t, n2 = z.shape
    d = n2 // 2
    tm = TM_NM
    return pl.pallas_call(
        _layernorm_kernel,
        grid=(t // tm,),
        in_specs=[
            pl.BlockSpec((tm, d), lambda m: (m, 1)),
            pl.BlockSpec((1, d), lambda m: (0, 0)),
            pl.BlockSpec((1, d), lambda m: (0, 0)),
        ],
        out_specs=pl.BlockSpec((tm, d), lambda m: (m, 0)),
        out_shape=jax.ShapeDtypeStruct((t, d), F32),
        compiler_params=_params(("arbitrary",)),
        name="layernorm_v",
    )(z, g.reshape(1, d), b.reshape(1, d))


def _gate_prompt_kernel(groups, u_ref, v_ref, ws_ref, bs_ref, o_ref):
    chunk = ws_ref.shape[1]
    hd = u_ref.shape[1] // groups
    row = lax.broadcasted_iota(jnp.int32, (chunk, chunk), 0)
    col = lax.broadcasted_iota(jnp.int32, (chunk, chunk), 1)
    for g in range(groups):
        ws = jnp.where(col <= row, ws_ref[g], 0.0).astype(BF16)
        vg = v_ref[:, g * hd:(g + 1) * hd].astype(BF16)
        mixed = _dot(ws, vg) + bs_ref[:, g:g + 1]
        o_ref[:, g * hd:(g + 1) * hd] = (u_ref[:, g * hd:(g + 1) * hd] * mixed).astype(o_ref.dtype)


def _gate_prompt(z, vn, w_s, b_s_t, n_prompt_rows):
    groups, chunk, _ = w_s.shape
    d = vn.shape[1]
    return pl.pallas_call(
        functools.partial(_gate_prompt_kernel, groups),
        grid=(n_prompt_rows // chunk,),
        in_specs=[
            pl.BlockSpec((chunk, d), lambda c: (c, 0)),
            pl.BlockSpec((chunk, d), lambda c: (c, 0)),
            pl.BlockSpec((groups, chunk, chunk), lambda c: (0, 0, 0)),
            pl.BlockSpec((chunk, groups), lambda c: (0, 0)),
        ],
        out_specs=pl.BlockSpec((chunk, d), lambda c: (c, 0)),
        out_shape=jax.ShapeDtypeStruct((n_prompt_rows, d), BF16),
        compiler_params=_params(("arbitrary",)),
        name="gate_prompt",
    )(z, vn, w_s, b_s_t)


def _gate_sample_kernel(n_pos, ws_ref, bs_ref, u_ref, v_ref, o_ref):
    g = pl.program_id(0)
    for t in range(n_pos):
        mixed = None
        for s in range(t + 1):
            term = ws_ref[g, t * n_pos + s] * v_ref[s * SLAB:(s + 1) * SLAB, :]
            mixed = term if mixed is None else mixed + term
        mixed = mixed + bs_ref[g, t]
        o_ref[t * SLAB:(t + 1) * SLAB, :] = (u_ref[t * SLAB:(t + 1) * SLAB, :] * mixed).astype(o_ref.dtype)


def _gate_sample(z, vn, ws_small, bs_small, n_prompt_rows, n_pos):
    groups = ws_small.shape[0]
    d = vn.shape[1]
    hd = d // groups
    rows = n_pos * SLAB
    blk = n_prompt_rows // rows
    return pl.pallas_call(
        functools.partial(_gate_sample_kernel, n_pos),
        grid=(groups,),
        in_specs=[
            pl.BlockSpec(memory_space=pltpu.SMEM),
            pl.BlockSpec(memory_space=pltpu.SMEM),
            pl.BlockSpec((rows, hd), lambda g: (blk, g)),
            pl.BlockSpec((rows, hd), lambda g: (blk, g)),
        ],
        out_specs=pl.BlockSpec((rows, hd), lambda g: (0, g)),
        out_shape=jax.ShapeDtypeStruct((rows, d), BF16),
        compiler_params=_params(("arbitrary",)),
        name="gate_sample",
    )(ws_small, bs_small, z, vn)


def _gather_kernel(total_ref, src_ref, nxt_ref, h_hbm, o_ref, buf, sem):
    i = pl.program_id(0)
    tm = buf.shape[1]
    total = total_ref[0]
    slot = i % 2

    def start(rows_ref, s):
        def issue(j, carry):
            r = rows_ref[0, 0, j]
            pltpu.make_async_copy(h_hbm.at[pl.ds(r, 1), :], buf.at[s, pl.ds(j, 1), :], sem.at[s]).start()
            return carry

        lax.fori_loop(0, tm, issue, 0, unroll=8)

    @pl.when(i == 0)
    def _():
        start(src_ref, 0)

    @pl.when(i + 1 < total)
    def _():
        start(nxt_ref, 1 - slot)

    @pl.when(i < total)
    def _():
        pltpu.make_async_copy(h_hbm.at[pl.ds(0, tm), :], buf.at[slot], sem.at[slot]).wait()
        o_ref[...] = buf[slot].astype(o_ref.dtype)

    @pl.when(i >= total)
    def _():
        o_ref[...] = jnp.zeros(o_ref.shape, o_ref.dtype)


def _gather_rows(h, src_rows, total_tiles, n_tiles):
    t, d = h.shape
    tm = TM_E
    return pl.pallas_call(
        _gather_kernel,
        grid_spec=pltpu.PrefetchScalarGridSpec(
            num_scalar_prefetch=1,
            grid=(n_tiles,),
            in_specs=[
                pl.BlockSpec((1, 1, tm), lambda i, tot: (i, 0, 0), memory_space=pltpu.SMEM),
                pl.BlockSpec((1, 1, tm), lambda i, tot: (jnp.minimum(i + 1, n_tiles - 1), 0, 0),
                             memory_space=pltpu.SMEM),
                pl.BlockSpec(memory_space=pl.ANY),
            ],
            out_specs=pl.BlockSpec((tm, d), lambda i, tot: (i, 0)),
            scratch_shapes=[pltpu.VMEM((2, tm, d), F32), pltpu.SemaphoreType.DMA((2,))],
        ),
        out_shape=jax.ShapeDtypeStruct((n_tiles * tm, d), BF16),
        compiler_params=_params(("arbitrary",)),
        name="moe_gather",
    )(total_tiles, src_rows.reshape(n_tiles, 1, tm), src_rows.reshape(n_tiles, 1, tm), h)


(F_E, F_N, F_M, F_OM, F_ON, F_VALID, F_FIRST, F_SLOT, F_PF_LO, F_PF_HI, F_NEXT_E, F_NEXT_N, N_FIELDS) = range(13)
K_CHUNK = 512
X_SPLIT = 4


def _grouped_matmul_kernel(layer, n_items, n_w, epilogue, tab, *refs):
    x_refs = refs[:X_SPLIT]
    refs = refs[X_SPLIT:]
    w_hbm = refs[:n_w]
    b_refs = refs[n_w:2 * n_w]
    o_ref = refs[2 * n_w]
    w_buf = refs[2 * n_w + 1:3 * n_w + 1]
    w_bf = refs[3 * n_w + 1:4 * n_w + 1]
    sem = refs[4 * n_w + 1]
    kx = x_refs[0].shape[1]
    s = pl.program_id(0)
    fld = lambda f: tab[f * n_items + s]
    kdim, tn = w_bf[0].shape
    valid = fld(F_VALID)
    first = fld(F_FIRST)
    slot = fld(F_SLOT)

    n_chunks = kdim // K_CHUNK

    def copies(e, n, sl, rows):
        col = pl.multiple_of(n * tn, tn)
        return [pltpu.make_async_copy(w_hbm[i].at[layer, e, rows, pl.ds(col, tn)], w_buf[i].at[sl, rows, :],
                                      sem.at[i, sl]) for i in range(n_w)]

    @pl.when(s == 0)
    def _():
        for c in copies(fld(F_E), fld(F_N), 0, pl.ds(0, kdim)):
            c.start()

    pf_lo = fld(F_PF_LO)
    pf_hi = fld(F_PF_HI)

    def prefetch_next():
        for c in range(n_chunks):
            @pl.when((pf_lo <= c) & (c < pf_hi))
            def _():
                for cp in copies(fld(F_NEXT_E), fld(F_NEXT_N), 1 - slot, pl.ds(c * K_CHUNK, K_CHUNK)):
                    cp.start()

    @pl.when((valid == 1) & (first == 1))
    def _():
        for c in copies(0, 0, slot, pl.ds(0, kdim)):
            c.wait()
        prefetch_next()
        acc = [None] * n_w
        for c in range(kdim // K_CHUNK):
            ks = slice(c * K_CHUNK, (c + 1) * K_CHUNK)
            k0 = c * K_CHUNK
            xk = x_refs[k0 // kx][:, k0 % kx:k0 % kx + K_CHUNK]
            for i in range(n_w):
                w = w_buf[i][slot, ks, :].astype(BF16)
                w_bf[i][ks, :] = w
                p = _dot(xk, w)
                acc[i] = p if acc[i] is None else acc[i] + p
        o_ref[...] = epilogue(*[acc[i] + b_refs[i][0] for i in range(n_w)]).astype(o_ref.dtype)

    @pl.when((valid == 1) & (first == 0))
    def _():
        prefetch_next()
        acc = [None] * n_w
        for q in range(X_SPLIT):
            xq = x_refs[q][...]
            for i in range(n_w):
                p = _dot(xq, w_bf[i][q * kx:(q + 1) * kx, :])
                acc[i] = p if acc[i] is None else acc[i] + p
        o_ref[...] = epilogue(*[acc[i] + b_refs[i][0] for i in range(n_w)]).astype(o_ref.dtype)

    @pl.when(valid == 0)
    def _():
        o_ref[...] = jnp.zeros(o_ref.shape, o_ref.dtype)


def _swiglu(g, u):
    g = jnp.minimum(g, SWIGLU_LIMIT)
    u = jnp.clip(u, -SWIGLU_LIMIT, SWIGLU_LIMIT)
    return (u + 1.0) * (g * jax.nn.sigmoid(SWIGLU_ALPHA * g))


def _grouped_matmul(x, weights, biases, layer, tab, n_tiles, tn, epilogue, out_dtype, name):
    depth, n_e, kdim, n_out = weights[0].shape
    n_w = len(weights)
    tm = TM_E
    n_items = n_tiles * (n_out // tn)
    bias_spec = pl.BlockSpec((1, 1, tn), lambda s, t: (layer * n_e + t[F_E * n_items + s], 0, t[F_N * n_items + s]))
    return pl.pallas_call(
        functools.partial(_grouped_matmul_kernel, layer, n_items, n_w, epilogue),
        grid_spec=pltpu.PrefetchScalarGridSpec(
            num_scalar_prefetch=1,
            grid=(n_items,),
            in_specs=[pl.BlockSpec((tm, kdim // X_SPLIT), functools.partial(lambda q, s, t: (t[F_M * n_items + s], q), q))
                      for q in range(X_SPLIT)]
            + [pl.BlockSpec(memory_space=pl.ANY)] * n_w
            + [bias_spec] * n_w,
            out_specs=pl.BlockSpec((tm, tn), lambda s, t: (t[F_OM * n_items + s], t[F_ON * n_items + s])),
            scratch_shapes=[pltpu.VMEM((2, kdim, tn), F32)] * n_w
            + [pltpu.VMEM((kdim, tn), BF16)] * n_w
            + [pltpu.SemaphoreType.DMA((n_w, 2))],
        ),
        out_shape=jax.ShapeDtypeStruct((n_tiles * tm, n_out), out_dtype),
        compiler_params=_params(("arbitrary",)),
        name=name,
    )(tab, *([x] * X_SPLIT), *weights, *[b.reshape(depth * n_e, 1, n_out) for b in biases])


def _combine_kernel(pos_ref, nxt_ref, x_ref, gate_ref, p_ref, y_hbm, o_ref, buf, sem):
    i = pl.program_id(0)
    tm = x_ref.shape[0]
    slot = i % 2

    def start(rows_ref, s):
        def issue(j, carry):
            for k in range(TOP_K):
                r = rows_ref[0, 0, j * TOP_K + k]
                pltpu.make_async_copy(y_hbm.at[pl.ds(r, 1), :], buf.at[s, k, pl.ds(j, 1), :], sem.at[s]).start()
            return carry

        lax.fori_loop(0, tm, issue, 0, unroll=4)

    @pl.when(i == 0)
    def _():
        start(pos_ref, 0)

    @pl.when(i + 1 < pl.num_programs(0))
    def _():
        start(nxt_ref, 1 - slot)

    for k in range(TOP_K):
        pltpu.make_async_copy(y_hbm.at[pl.ds(0, tm), :], buf.at[slot, k], sem.at[slot]).wait()
    f = p_ref[:, 0:1] * buf[slot, 0]
    for k in range(1, TOP_K):
        f = f + p_ref[:, k:k + 1] * buf[slot, k]
    o_ref[...] = x_ref[...] + gate_ref[...] * f


def _combine(x, y, pos, probs, mods, k_gate, n_prompt_rows, seq_len):
    t, d = x.shape
    tm = TM_C
    rb = _mod_rowblock(tm, n_prompt_rows, seq_len)
    return pl.pallas_call(
        _combine_kernel,
        grid=(t // tm,),
        in_specs=[
            pl.BlockSpec((1, 1, tm * TOP_K), lambda m: (m, 0, 0), memory_space=pltpu.SMEM),
            pl.BlockSpec((1, 1, tm * TOP_K), lambda m: (jnp.minimum(m + 1, t // tm - 1), 0, 0),
                         memory_space=pltpu.SMEM),
            pl.BlockSpec((tm, d), lambda m: (m, 0)),
            pl.BlockSpec((SLAB, d), lambda m: (rb(m), k_gate)),
            pl.BlockSpec((tm, LANES), lambda m: (m, 0)),
            pl.BlockSpec(memory_space=pl.ANY),
        ],
        out_specs=pl.BlockSpec((tm, d), lambda m: (m, 0)),
        out_shape=jax.ShapeDtypeStruct((t, d), F32),
        scratch_shapes=[pltpu.VMEM((2, TOP_K, tm, d), F32), pltpu.SemaphoreType.DMA((2,))],
        compiler_params=_params(("arbitrary",)),
        name="moe_combine",
    )(pos.reshape(t // tm, 1, tm * TOP_K), pos.reshape(t // tm, 1, tm * TOP_K), x, mods, probs, y)


def _routing_tables(top_idx, rank, counts, n_tiles, n_chunks, nb_in, nb_out):
    t = top_idx.shape[0]
    n_experts = counts.shape[0]
    ar = jnp.arange(n_experts, dtype=jnp.int32)
    tiles_e = (counts + TM_E - 1) // TM_E
    tile_end = jnp.cumsum(tiles_e)
    tile_start = tile_end - tiles_e
    total_tiles = tile_end[-1]
    group_row = jnp.sum(jnp.where(top_idx[:, :, None] == ar, tile_start * TM_E, 0), axis=-1)
    pos = (group_row + rank).reshape(-1)
    src_rows = jnp.zeros((n_tiles * TM_E,), jnp.int32).at[pos].set(
        jnp.arange(t * TOP_K, dtype=jnp.int32) // TOP_K)

    used = tiles_e > 0
    used_before = jnp.cumsum(used.astype(jnp.int32)) - used.astype(jnp.int32)
    next_used = jnp.min(jnp.where((ar[None, :] > ar[:, None]) & used[None, :], ar[None, :], n_experts), axis=1)

    def items(nb):
        per_e = tiles_e * nb
        item_end = jnp.cumsum(per_e)
        item_start = item_end - per_e
        total = item_end[-1]
        s = jnp.arange(n_tiles * nb, dtype=jnp.int32)
        valid = s < total
        sc = jnp.minimum(s, total - 1)
        e = jnp.minimum(jnp.sum((item_end[None, :] <= sc[:, None]).astype(jnp.int32), axis=1), n_experts - 1)
        local = sc - item_start[e]
        ce = jnp.maximum(tiles_e[e], 1)
        n_in = local // ce
        m_in = tile_start[e] + local % ce
        spare = jnp.maximum(s - total, 0)
        m_out = jnp.where(valid, m_in, total_tiles + spare // nb)
        n_out = jnp.where(valid, n_in, spare % nb)
        first = valid & (local % ce == 0)
        slot = (used_before[e] * nb + n_in) % 2
        same_e = n_in + 1 < nb
        has_next = valid & (same_e | (next_used[e] < n_experts))
        next_e = jnp.where(same_e, e, jnp.minimum(next_used[e], n_experts - 1))
        next_n = jnp.where(same_e, n_in + 1, 0)
        j = local % ce
        pf_lo = jnp.where(has_next, j * n_chunks // ce, 0)
        pf_hi = jnp.where(has_next, (j + 1) * n_chunks // ce, 0)
        fields = [None] * N_FIELDS
        fields[F_E], fields[F_N], fields[F_M], fields[F_OM], fields[F_ON] = e, n_in, m_in, m_out, n_out
        fields[F_VALID], fields[F_FIRST], fields[F_SLOT] = valid, first, slot
        fields[F_PF_LO], fields[F_PF_HI], fields[F_NEXT_E], fields[F_NEXT_N] = pf_lo, pf_hi, next_e, next_n
        return jnp.concatenate([f.astype(jnp.int32) for f in fields])

    return pos.astype(jnp.int32), src_rows, total_tiles.reshape(1).astype(jnp.int32), items(nb_in), items(nb_out)


def _moe(x, g, mods, rb, n_prompt_rows, seq_len, layer, w_router, b_router, w_gate, b_gate, w_up, b_up, w_down, b_down):
    t, d = x.shape
    n_experts = w_router.shape[1]
    n_tiles = t * TOP_K // TM_E + n_experts
    tn_in, tn_out = 512, 1024
    assert w_gate.shape[2] == d and w_down.shape[2] == d
    h, idx, probs, rank, counts = _normmod_router(x, g, mods, 3, 4, rb, w_router, b_router)
    pos, src_rows, total_tiles, tab_in, tab_out = _routing_tables(
        idx[:, :TOP_K], rank[:, :TOP_K], counts[0, :n_experts], n_tiles, d // K_CHUNK,
        w_gate.shape[3] // tn_in, d // tn_out)
    xs = _gather_rows(h, src_rows, total_tiles, n_tiles)
    a = _grouped_matmul(xs, [w_gate, w_up], [b_gate, b_up], layer, tab_in, n_tiles, tn_in, _swiglu, BF16, "expert_in")
    y = _grouped_matmul(a, [w_down], [b_down], layer, tab_out, n_tiles, tn_out, lambda v: v, F32, "expert_out")
    return _combine(x, y, pos, probs, mods, 5, n_prompt_rows, seq_len)


def kernel(x_prompt, x_sample, c_prompt, c_sample, state_conv, norm_mix_g, norm_ffn_g, ada_w, ada_b, conv_w_in, conv_w, conv_w_out, sg_w_proj, sg_b_proj, sg_ln_g, sg_ln_b, sg_w_s, sg_b_s, sg_w_out, router_w, router_b, exp_w_gate, exp_b_gate, exp_w_up, exp_b_up, exp_w_down, exp_b_down, final_norm_g):
    bp, sp, d = x_prompt.shape
    bs, ss, _ = x_sample.shape
    depth = norm_mix_g.shape[0]
    n_prompt_rows = bp * sp
    n_sample_rows = bs * ss
    assert bs == SLAB and n_sample_rows == TM and sp % TM == 0 and sg_w_s.shape[2] == SLAB

    x = jnp.concatenate([x_prompt.reshape(n_prompt_rows, d),
                         x_sample.transpose(1, 0, 2).reshape(n_sample_rows, d)], axis=0)

    n_c = bp + bs
    c_all = jnp.pad(jnp.concatenate([c_prompt, c_sample], axis=0), ((0, -n_c % 8), (0, 0)))
    mods_c = _ada(c_all, ada_w, ada_b)
    mods_all = jnp.concatenate(
        [jnp.repeat(mods_c[:, :bp], SLAB, axis=1), mods_c[:, bp:n_c]], axis=1)

    rb_nm = _mod_rowblock(TM_NM, n_prompt_rows, sp)
    rb_mm = _mod_rowblock(TM, n_prompt_rows, sp)

    conv_new_p, conv_new_s, chunk_new_s = [], [], []
    for i in range(depth):
        mods = mods_all[i]
        j = i // 2
        h = _normmod(x, norm_mix_g[i], mods, 0, 1, rb_nm, BF16)
        if i % 2 == 0:
            state = state_conv[j].transpose(1, 0, 2)
            u, cz = _conv_in(h, conv_w_in[j].astype(BF16), conv_w[j], state, n_prompt_rows, sp)
            x = _mm_res(u, conv_w_out[j].astype(BF16), x, mods, 2, rb_mm)
            conv_new_p.append(cz[:n_prompt_rows].reshape(bp, sp, d)[:, sp - 2:])
            conv_new_s.append(cz[n_prompt_rows:].reshape(ss, bs, d)[ss - 2:].transpose(1, 0, 2))
        else:
            z = _sg_proj(h, sg_w_proj[j].astype(BF16), sg_b_proj[j])
            vn = _layernorm_v(z, sg_ln_g[j], sg_ln_b[j])
            gp = _gate_prompt(z, vn, sg_w_s[j], sg_b_s[j].T, n_prompt_rows)
            ws_small = sg_w_s[j][:, :ss, :ss].reshape(-1, ss * ss)
            gs = _gate_sample(z, vn, ws_small, sg_b_s[j][:, :ss], n_prompt_rows, ss)
            gated = jnp.concatenate([gp, gs], axis=0)
            x = _mm_res(gated, sg_w_out[j].astype(BF16), x, mods, 2, rb_mm)
            chunk_new_s.append(vn[n_prompt_rows:].reshape(ss, bs, d).transpose(1, 0, 2))
        x = _moe(x, norm_ffn_g[i], mods, rb_nm, n_prompt_rows, sp, i, router_w[i], router_b[i],
                 exp_w_gate, exp_b_gate, exp_w_up, exp_b_up, exp_w_down, exp_b_down)

    y = _final_norm(x, final_norm_g)
    y_prompt = y[:n_prompt_rows].reshape(bp, sp, d)
    y_sample = y[n_prompt_rows:].reshape(ss, bs, d).transpose(1, 0, 2)
    return (y_prompt, y_sample, jnp.stack(conv_new_p), jnp.stack(conv_new_s), jnp.stack(chunk_new_s))
```
